```python
import jax, jax.numpy as jnp
from jax import lax
import numpy as np

D_MODEL = 1024
BATCH = 8
SEQ = 2048
DEPTH = 2
DEC_BATCH = 128
DEC_SEQ = 4
PAST_LEN = 2048
PAGE_SIZE = 128

MIX_WIDTH = D_MODEL
SB_HEAD_DIM = 64
SB_WIDTH = MIX_WIDTH // 2
SB_HEADS = SB_WIDTH // SB_HEAD_DIM
GM_GROUP_DIM = 64
GM_WIDTH = MIX_WIDTH - SB_WIDTH
GM_GROUPS = GM_WIDTH // GM_GROUP_DIM
CHUNK = 128
Q_BLOCK = 128
D_FF = 4 * D_MODEL
IN_WIDTH = 3 * SB_WIDTH + 2 * GM_WIDTH
SB_BIAS_INIT = -6.0
EPS = 1e-6

kernel_name = "hybrid_stickbreak_chunkmlp_decode_step"


def rmsnorm(x, g):
    xf = x.astype(jnp.float32)
    xf = xf * lax.rsqrt(jnp.mean(xf * xf, axis=-1, keepdims=True) + EPS)
    return (xf * g.astype(jnp.float32)).astype(x.dtype)


def adaln(c, w_ada, b_ada):
    m = jax.nn.silu(c) @ w_ada + b_ada
    return [t[:, None, :] for t in jnp.split(m, 6, axis=-1)]


def sb_attend(q, k, v, sb_bias, q_pos, k_pos):
    z = jnp.einsum("bqhd,bkhd->bhqk", q.astype(jnp.float32), k.astype(jnp.float32)) * (SB_HEAD_DIM ** -0.5)
    z = z + sb_bias.astype(jnp.float32)[None, :, None, None]
    causal = (k_pos[None, :] < q_pos[:, None])[None, None]
    sp = jnp.where(causal, jax.nn.softplus(z), 0.0)
    tail = lax.cumsum(sp, axis=3, reverse=True)
    a = jnp.exp(jnp.where(causal, z - tail, -jnp.inf))
    return jnp.einsum("bhqk,bkhd->bqhd", a, v.astype(jnp.float32)).astype(v.dtype)


def sb_prompt(q, k, v, sb_bias):
    b, t, h, d = q.shape
    nb = t // Q_BLOCK
    qb = jnp.swapaxes(q.reshape(b, nb, Q_BLOCK, h, d), 0, 1)
    k_pos = jnp.arange(t)

    def block(args):
        qi, i = args
        return sb_attend(qi, k, v, sb_bias, i * Q_BLOCK + jnp.arange(Q_BLOCK), k_pos)

    out = lax.map(block, (qb, jnp.arange(nb)))
    return jnp.swapaxes(out, 0, 1).reshape(b, t, h, d)


def spatial_gate(gv, w_s, b_s):
    n = gv.shape[2]
    w = w_s[:, :n, :n] * jnp.tril(jnp.ones((n, n), w_s.dtype))
    bias = jnp.swapaxes(b_s[:, :n], 0, 1)[None, None, :, :, None]
    return jnp.einsum("gts,bcsgd->bctgd", w, gv) + bias


def layer(x, mods, k_past, v_past, g_mix, w_in, sb_bias, gm_vnorm, w_s, b_s, gn_sb, gn_gm,
          w_out, g_ff, w_ff1, w_ff2):
    sh1, sc1, gt1, sh2, sc2, gt2 = mods
    b, t, _ = x.shape
    h = rmsnorm(x, g_mix) * (1 + sc1) + sh1
    p = h @ w_in
    q, k, v, u, gv = jnp.split(
        p, [SB_WIDTH, 2 * SB_WIDTH, 3 * SB_WIDTH, 3 * SB_WIDTH + GM_WIDTH], axis=-1)
    q = q.reshape(b, t, SB_HEADS, SB_HEAD_DIM)
    k = k.reshape(b, t, SB_HEADS, SB_HEAD_DIM)
    v = v.reshape(b, t, SB_HEADS, SB_HEAD_DIM)
    gv = rmsnorm(gv, gm_vnorm).reshape(b, t, GM_GROUPS, GM_GROUP_DIM)
    if k_past is None:
        o_sb = sb_prompt(q, k, v, sb_bias)
        mixed = spatial_gate(gv.reshape(b, t // CHUNK, CHUNK, GM_GROUPS, GM_GROUP_DIM), w_s, b_s)
    else:
        past = k_past.shape[1]
        k_all = jnp.concatenate([k_past.astype(k.dtype), k], axis=1)
        v_all = jnp.concatenate([v_past.astype(v.dtype), v], axis=1)
        o_sb = sb_attend(q, k_all, v_all, sb_bias, past + jnp.arange(t), jnp.arange(past + t))
        mixed = spatial_gate(gv[:, None], w_s, b_s)
    o_sb = rmsnorm(o_sb.reshape(b, t, SB_WIDTH), gn_sb)
    o_gm = rmsnorm(u * mixed.reshape(b, t, GM_WIDTH), gn_gm)
    x = x + gt1 * (jnp.concatenate([o_sb, o_gm], axis=-1) @ w_out)
    h = rmsnorm(x, g_ff) * (1 + sc2) + sh2
    x = x + gt2 * (jnp.square(jax.nn.relu(h @ w_ff1)) @ w_ff2)
    return x, k, v, gv


def setup_inputs(seed: int = 0) -> dict:
    key = jax.random.key(seed)
    ks = jax.random.split(key, 24)
    n_pages = PAST_LEN // PAGE_SIZE
    n_used = DEC_BATCH * n_pages
    n_pool = n_used + n_used // 4

    def nrm(k, shape, s):
        return jax.random.normal(k, shape, jnp.float32) * s

    page_table = jax.random.permutation(ks[0], n_pool)[:n_used].reshape(DEC_BATCH, n_pages).astype(jnp.int32)
    cache_shape = (DEPTH, n_pool, PAGE_SIZE, SB_HEADS, SB_HEAD_DIM)
    return {
        "x_prompt": nrm(ks[1], (BATCH, SEQ, D_MODEL), 1.0),
        "x_sample": nrm(ks[2], (DEC_BATCH, DEC_SEQ, D_MODEL), 1.0),
        "cache_k": nrm(ks[3], cache_shape, 1.0),
        "cache_v": nrm(ks[4], cache_shape, 1.0),
        "page_table": page_table,
        "c_prompt": nrm(ks[5], (BATCH, D_MODEL), 1.0),
        "c_sample": nrm(ks[6], (DEC_BATCH, D_MODEL), 1.0),
        "w_ada": nrm(ks[7], (DEPTH, D_MODEL, 6 * D_MODEL), D_MODEL ** -0.5),
        "b_ada": nrm(ks[8], (DEPTH, 6 * D_MODEL), 0.02),
        "g_mix": 1.0 + nrm(ks[9], (DEPTH, D_MODEL), 0.02),
        "w_in": nrm(ks[10], (DEPTH, D_MODEL, IN_WIDTH), D_MODEL ** -0.5),
        "sb_bias": SB_BIAS_INIT + nrm(ks[21], (DEPTH, SB_HEADS), 0.1),
        "gm_vnorm": 1.0 + nrm(ks[11], (DEPTH, GM_WIDTH), 0.02),
        "w_s": nrm(ks[12], (DEPTH, GM_GROUPS, CHUNK, CHUNK), CHUNK ** -0.5),
        "b_s": 1.0 + nrm(ks[13], (DEPTH, GM_GROUPS, CHUNK), 0.02),
        "gn_sb": 1.0 + nrm(ks[14], (DEPTH, SB_WIDTH), 0.02),
        "gn_gm": 1.0 + nrm(ks[15], (DEPTH, GM_WIDTH), 0.02),
        "w_out": nrm(ks[16], (DEPTH, MIX_WIDTH, D_MODEL), MIX_WIDTH ** -0.5),
        "g_ff": 1.0 + nrm(ks[17], (DEPTH, D_MODEL), 0.02),
        "w_ff1": nrm(ks[18], (DEPTH, D_MODEL, D_FF), D_MODEL ** -0.5),
        "w_ff2": nrm(ks[19], (DEPTH, D_FF, D_MODEL), D_FF ** -0.5),
        "g_final": 1.0 + nrm(ks[20], (D_MODEL,), 0.02),
    }


def reference(x_prompt, x_sample, cache_k, cache_v, page_table, c_prompt, c_sample,
              w_ada, b_ada, g_mix, w_in, sb_bias, gm_vnorm, w_s, b_s, gn_sb, gn_gm, w_out,
              g_ff, w_ff1, w_ff2, g_final):
    n_seq, n_pages = page_table.shape
    past_len = n_pages * PAGE_SIZE
    xp, xs = x_prompt, x_sample
    kp, vp, gp, ksm, vsm, gsm = [], [], [], [], [], []
    for l in range(DEPTH):
        wts = (g_mix[l], w_in[l], sb_bias[l], gm_vnorm[l], w_s[l], b_s[l], gn_sb[l], gn_gm[l],
               w_out[l], g_ff[l], w_ff1[l], w_ff2[l])
        xp, k, v, gv = layer(xp, adaln(c_prompt, w_ada[l], b_ada[l]), None, None, *wts)
        kp.append(k.reshape(xp.shape[0], -1, PAGE_SIZE, SB_HEADS, SB_HEAD_DIM))
        vp.append(v.reshape(xp.shape[0], -1, PAGE_SIZE, SB_HEADS, SB_HEAD_DIM))
        gp.append(gv[:, -CHUNK:])
        k_past = cache_k[l][page_table].reshape(n_seq, past_len, SB_HEADS, SB_HEAD_DIM)
        v_past = cache_v[l][page_table].reshape(n_seq, past_len, SB_HEADS, SB_HEAD_DIM)
        xs, k, v, gv = layer(xs, adaln(c_sample, w_ada[l], b_ada[l]), k_past, v_past, *wts)
        ksm.append(k)
        vsm.append(v)
        gsm.append(gv)
    y_prompt = rmsnorm(xp, g_final)
    y_sample = rmsnorm(xs, g_final)
    return (y_prompt, y_sample, jnp.stack(kp), jnp.stack(vp), jnp.stack(gp),
            jnp.stack(ksm), jnp.stack(vsm), jnp.stack(gsm))
```

```python
import functools

import jax
import jax.numpy as jnp
from jax import lax
from jax.experimental import pallas as pl
from jax.experimental.pallas import tpu as pltpu

D_MODEL = 1024
SB_WIDTH = 512
GM_WIDTH = 512
HEAD_DIM = 64
N_HEADS = 8
N_PAIRS = 4
IN_WIDTH = 3 * SB_WIDTH + 2 * GM_WIDTH
D_FF = 4 * D_MODEL
CHUNK = 128
DEC_SEQ = 4
EPS = 1e-6
LANES = 128
VMEM_LIMIT = 56 * 1024 * 1024

F32 = jnp.float32
BF16 = jnp.bfloat16


def _rms(x, g):
    ms = jnp.mean(x * x, axis=-1, keepdims=True)
    return x * lax.rsqrt(ms + EPS) * g


def _softplus(z):
    return jnp.maximum(z, 0.0) + jnp.log(1.0 + jnp.exp(-jnp.abs(z)))


def _split_bf16(x):
    hi = x.astype(BF16)
    lo = (x - hi.astype(F32)).astype(BF16)
    return jnp.concatenate([hi, lo], axis=1)


def _dot(a, b):
    return jnp.dot(a, b, preferred_element_type=F32)


def _dot_nt(a, b):
    return lax.dot_general(a, b, (((1,), (1,)), ((), ())), preferred_element_type=F32)


def _ada_kernel(c_ref, w_ref, b_ref, o_ref):
    c = c_ref[...]
    s = c * jax.nn.sigmoid(c)
    o_ref[0] = _dot(s.astype(BF16), w_ref[0].astype(BF16)) + b_ref[0]


def _ada_call(c_all, w_ada, b_ada):
    depth, _, n_out = w_ada.shape
    rows = c_all.shape[0]
    tn = 1024
    return pl.pallas_call(
        _ada_kernel,
        grid=(depth, n_out // tn),
        in_specs=[
            pl.BlockSpec((rows, D_MODEL), lambda l, j: (0, 0)),
            pl.BlockSpec((1, D_MODEL, tn), lambda l, j: (l, 0, j)),
            pl.BlockSpec((1, 1, tn), lambda l, j: (l, 0, j)),
        ],
        out_specs=pl.BlockSpec((1, rows, tn), lambda l, j: (l, 0, j)),
        out_shape=jax.ShapeDtypeStruct((depth, rows, n_out), F32),
        compiler_params=pltpu.CompilerParams(
            dimension_semantics=("arbitrary", "arbitrary"), vmem_limit_bytes=VMEM_LIMIT),
        name="ada",
    )(c_all, w_ada, b_ada.reshape(depth, 1, n_out))


def _pre_kernel(x_ref, m_ref, gmix_ref, win_ref, gvnorm_ref,
                q_ref, k_ref, v_ref, kb_ref, vb_ref, u_ref, gvn_ref):
    x = x_ref[...]
    h = _rms(x, gmix_ref[...]) * (1.0 + m_ref[1]) + m_ref[0]
    p = _dot(h.astype(BF16), win_ref[...])
    q_ref[...] = (p[:, 0:SB_WIDTH] * (HEAD_DIM ** -0.5)).astype(BF16)
    k = p[:, SB_WIDTH:2 * SB_WIDTH]
    v = p[:, 2 * SB_WIDTH:3 * SB_WIDTH]
    k_ref[...] = k
    v_ref[...] = v
    kb_ref[...] = k.astype(BF16)
    vb_ref[...] = v.astype(BF16)
    u_ref[...] = p[:, 3 * SB_WIDTH:3 * SB_WIDTH + GM_WIDTH]
    gvn_ref[...] = _rms(p[:, 3 * SB_WIDTH + GM_WIDTH:], gvnorm_ref[...])


def _pre_call(x, mods, g_mix, w_in_b, gm_vnorm, tm):
    nb, t, _ = x.shape
    r = mods.shape[2]
    tok = lambda w: pl.BlockSpec((None, tm, w), lambda b, i: (b, i, 0))
    const = lambda shape: pl.BlockSpec(shape, lambda b, i: (0,) * len(shape))
    widths = [(SB_WIDTH, BF16), (SB_WIDTH, F32), (SB_WIDTH, F32), (SB_WIDTH, BF16),
              (SB_WIDTH, BF16), (GM_WIDTH, F32), (GM_WIDTH, F32)]
    return pl.pallas_call(
        _pre_kernel,
        grid=(nb, t // tm),
        in_specs=[
            tok(D_MODEL),
            pl.BlockSpec((None, 6, r, D_MODEL), lambda b, i: (b, 0, 0, 0)),
            const((1, D_MODEL)),
            const((D_MODEL, IN_WIDTH)),
            const((1, GM_WIDTH)),
        ],
        out_specs=[tok(w) for w, _ in widths],
        out_shape=[jax.ShapeDtypeStruct((nb, t, w), dt) for w, dt in widths],
        compiler_params=pltpu.CompilerParams(
            dimension_semantics=("arbitrary", "arbitrary"), vmem_limit_bytes=VMEM_LIMIT),
        name="pre",
    )(x, mods, g_mix.reshape(1, D_MODEL), w_in_b, gm_vnorm.reshape(1, GM_WIDTH))


def _sb_block(q2, kblk, vblk, bias, uu, carry, mask, keys_on_lanes=False):
    z = (_dot(q2, kblk) if keys_on_lanes else _dot_nt(q2, kblk)) + bias
    sp = _softplus(z)
    if mask is not None:
        sp = jnp.where(mask, sp, 0.0)
    r = _dot(_split_bf16(sp), uu)
    tail = r[:, :LANES]
    if carry is not None:
        tail = tail + carry
    e = jnp.exp(z - tail)
    if mask is not None:
        e = jnp.where(mask, e, 0.0)
    new_carry = r[:, LANES:] if carry is None else carry + r[:, LANES:]
    e = e.astype(BF16)
    return (_dot_nt(e, vblk) if keys_on_lanes else _dot(e, vblk)), new_carry


def _attn_prompt_kernel(q_ref, k_ref, v_ref, bias_ref, uu_ref, o_ref, q2_ref, acc_ref, carry_ref):
    qi = pl.program_id(1)
    lane = lax.broadcasted_iota(jnp.int32, (CHUNK, LANES), 1)
    row = lax.broadcasted_iota(jnp.int32, (CHUNK, LANES), 0)
    first_head = lane < HEAD_DIM
    causal = lane < row
    causal2 = jnp.concatenate([causal, causal], axis=0)

    qf = q_ref[...].astype(F32)
    for hp in range(N_PAIRS):
        qp = qf[:, hp * LANES:(hp + 1) * LANES]
        q2_ref[hp] = jnp.concatenate(
            [jnp.where(first_head, qp, 0.0), jnp.where(first_head, 0.0, qp)], axis=0).astype(BF16)

    def block(kb, first):
        start = pl.multiple_of(kb * CHUNK, CHUNK)
        ks = k_ref[pl.ds(start, CHUNK), :]
        vs = v_ref[pl.ds(start, CHUNK), :]
        for hp in range(N_PAIRS):
            cols = slice(hp * LANES, (hp + 1) * LANES)
            pv, carry = _sb_block(q2_ref[hp], ks[:, cols], vs[:, cols], bias_ref[hp], uu_ref[...],
                                  None if first else carry_ref[hp], causal2 if first else None)
            carry_ref[hp] = carry
            acc_ref[hp] = pv if first else acc_ref[hp] + pv

    block(qi, True)

    def body(i, c):
        block(qi - 1 - i, False)
        return c

    lax.fori_loop(0, qi, body, 0)

    for hp in range(N_PAIRS):
        a = acc_ref[hp]
        o_ref[:, hp * LANES:(hp + 1) * LANES] = jnp.where(first_head, a[:CHUNK], a[CHUNK:])


def _attn_prompt_call(qb, kb, vb, bias_p, uu):
    nb, t, _ = qb.shape
    return pl.pallas_call(
        _attn_prompt_kernel,
        grid=(nb, t // CHUNK),
        in_specs=[
            pl.BlockSpec((None, CHUNK, SB_WIDTH), lambda b, i: (b, i, 0)),
            pl.BlockSpec((None, t, SB_WIDTH), lambda b, i: (b, 0, 0)),
            pl.BlockSpec((None, t, SB_WIDTH), lambda b, i: (b, 0, 0)),
            pl.BlockSpec((N_PAIRS, 2 * CHUNK, LANES), lambda b, i: (0, 0, 0)),
            pl.BlockSpec((2 * LANES, 2 * LANES), lambda b, i: (0, 0)),
        ],
        out_specs=pl.BlockSpec((None, CHUNK, SB_WIDTH), lambda b, i: (b, i, 0)),
        out_shape=jax.ShapeDtypeStruct((nb, t, SB_WIDTH), F32),
        scratch_shapes=[
            pltpu.VMEM((N_PAIRS, 2 * CHUNK, LANES), BF16),
            pltpu.VMEM((N_PAIRS, 2 * CHUNK, LANES), F32),
            pltpu.VMEM((N_PAIRS, 2 * CHUNK, LANES), F32),
        ],
        compiler_params=pltpu.CompilerParams(
            dimension_semantics=("arbitrary", "arbitrary"), vmem_limit_bytes=VMEM_LIMIT),
        name="attn_prompt",
    )(qb, kb, vb, bias_p, uu)


def _attn_sample_kernel(pt_ref, q_ref, kn_ref, vn_ref, kc_ref, vc_ref, bias_ref, uu_ref, o_ref,
                        q2_ref, acc_ref, carry_ref, kpad_ref, vpad_ref):
    j = pl.program_id(1)
    rows = DEC_SEQ * N_HEADS

    @pl.when(j == 0)
    def _():
        lane = lax.broadcasted_iota(jnp.int32, (rows, SB_WIDTH), 1)
        row = lax.broadcasted_iota(jnp.int32, (rows, SB_WIDTH), 0)
        qf = q_ref[...].astype(F32)
        qrep = jnp.broadcast_to(qf[:, None, :], (DEC_SEQ, N_HEADS, SB_WIDTH)).reshape(rows, SB_WIDTH)
        q2_ref[...] = jnp.where(lane // HEAD_DIM == row % N_HEADS, qrep, 0.0).astype(BF16)
        kpad_ref[...] = jnp.zeros_like(kpad_ref)
        vpad_ref[...] = jnp.zeros_like(vpad_ref)
        kpad_ref[0:DEC_SEQ, :] = kn_ref[...]
        vpad_ref[0:DEC_SEQ, :] = vn_ref[...]
        col = lax.broadcasted_iota(jnp.int32, (rows, LANES), 1)
        qpos = lax.broadcasted_iota(jnp.int32, (rows, LANES), 0) // N_HEADS
        pv, carry = _sb_block(q2_ref[...], kpad_ref[...].astype(BF16), vpad_ref[...].astype(BF16),
                              bias_ref[...], uu_ref[...], None, col < qpos)
        carry_ref[...] = carry
        acc_ref[...] = pv

    pv, carry = _sb_block(q2_ref[...], kc_ref[...].astype(BF16), vc_ref[...].astype(BF16),
                          bias_ref[...], uu_ref[...], carry_ref[...], None, keys_on_lanes=True)
    carry_ref[...] = carry
    acc_ref[...] += pv

    @pl.when(j == pl.num_programs(1) - 1)
    def _():
        lane = lax.broadcasted_iota(jnp.int32, (rows, SB_WIDTH), 1)
        row = lax.broadcasted_iota(jnp.int32, (rows, SB_WIDTH), 0)
        own = jnp.where(lane // HEAD_DIM == row % N_HEADS, acc_ref[...], 0.0)
        o_ref[...] = jnp.sum(own.reshape(DEC_SEQ, N_HEADS, SB_WIDTH), axis=1)


def _attn_sample_call(page_table, qb, k_new, v_new, cache_k, cache_v, layer, bias_s, uu):
    n_seq, n_pages = page_table.shape
    rows = DEC_SEQ * N_HEADS
    new = lambda: pl.BlockSpec((None, DEC_SEQ, SB_WIDTH), lambda b, j, pt: (b, 0, 0))
    page = lambda: pl.BlockSpec((None, None, SB_WIDTH, CHUNK),
                                lambda b, j, pt: (layer, pt[b, n_pages - 1 - j], 0, 0))
    grid_spec = pltpu.PrefetchScalarGridSpec(
        num_scalar_prefetch=1,
        grid=(n_seq, n_pages),
        in_specs=[
            new(), new(), new(), page(), page(),
            pl.BlockSpec((rows, LANES), lambda b, j, pt: (0, 0)),
            pl.BlockSpec((2 * LANES, 2 * LANES), lambda b, j, pt: (0, 0)),
        ],
        out_specs=pl.BlockSpec((None, DEC_SEQ, SB_WIDTH), lambda b, j, pt: (b, 0, 0)),
        scratch_shapes=[
            pltpu.VMEM((rows, SB_WIDTH), BF16),
            pltpu.VMEM((rows, SB_WIDTH), F32),
            pltpu.VMEM((rows, LANES), F32),
            pltpu.VMEM((CHUNK, SB_WIDTH), F32),
            pltpu.VMEM((CHUNK, SB_WIDTH), F32),
        ],
    )
    return pl.pallas_call(
        _attn_sample_kernel,
        grid_spec=grid_spec,
        out_shape=jax.ShapeDtypeStruct((n_seq, DEC_SEQ, SB_WIDTH), F32),
        compiler_params=pltpu.CompilerParams(
            dimension_semantics=("arbitrary", "arbitrary"), vmem_limit_bytes=VMEM_LIMIT),
        name="attn_sample",
    )(page_table, qb, k_new, v_new, cache_k, cache_v, bias_s, uu)


def _gate_prompt(gvn, ws_ref, gbias):
    tm = gvn.shape[0]
    lane = lax.broadcasted_iota(jnp.int32, (CHUNK, LANES), 1)
    row = lax.broadcasted_iota(jnp.int32, (CHUNK, LANES), 0)
    first_group = lane < HEAD_DIM
    keep = lane <= row
    out_cols = []
    for gp in range(N_PAIRS):
        w_pair = jnp.concatenate(
            [jnp.where(keep, ws_ref[2 * gp], 0.0), jnp.where(keep, ws_ref[2 * gp + 1], 0.0)],
            axis=1).astype(BF16)
        chunks = []
        for c in range(tm // CHUNK):
            r = gvn[c * CHUNK:(c + 1) * CHUNK, gp * LANES:(gp + 1) * LANES]
            r2 = jnp.concatenate(
                [jnp.where(first_group, r, 0.0), jnp.where(first_group, 0.0, r)], axis=0).astype(BF16)
            chunks.append(_dot(w_pair, r2) + gbias[:, gp * LANES:(gp + 1) * LANES])
        out_cols.append(jnp.concatenate(chunks, axis=0))
    return jnp.concatenate(out_cols, axis=1)


def _gate_sample(gvn, pat_ref):
    tm = gvn.shape[0]
    g3 = gvn.reshape(tm // 8, 8, GM_WIDTH)
    out = g3 * pat_ref[0][None] + pat_ref[DEC_SEQ][None]
    for k in range(1, DEC_SEQ):
        out = out + pltpu.roll(g3, k, axis=1) * pat_ref[k][None]
    return out.reshape(tm, GM_WIDTH)


def _post_kernel(x_ref, m_ref, osb_ref, u_ref, gvn_ref, gate_w_ref, gate_b_ref, gnsb_ref, gngm_ref,
                 wout_ref, gff_ref, w1_ref, w2_ref, gfin_ref, *out_refs, sample, final):
    x = x_ref[...]
    o_sb = _rms(osb_ref[...], gnsb_ref[...])
    if sample:
        mixed = _gate_sample(gvn_ref[...], gate_w_ref)
    else:
        mixed = _gate_prompt(gvn_ref[...], gate_w_ref, gate_b_ref[...])
    o_gm = _rms(u_ref[...] * mixed, gngm_ref[...])
    cat = jnp.concatenate([o_sb, o_gm], axis=-1).astype(BF16)
    x = x + m_ref[2] * _dot(cat, wout_ref[...])
    h = (_rms(x, gff_ref[...]) * (1.0 + m_ref[4]) + m_ref[3]).astype(BF16)
    ff = None
    n_split = 4
    cw = D_FF // n_split
    for c in range(n_split):
        a = jnp.maximum(_dot(h, w1_ref[:, c * cw:(c + 1) * cw]), 0.0)
        part = _dot((a * a).astype(BF16), w2_ref[c * cw:(c + 1) * cw, :])
        ff = part if ff is None else ff + part
    x = x + m_ref[5] * ff
    out_refs[0][...] = x
    if final:
        out_refs[1][...] = _rms(x, gfin_ref[...])


def _post_call(x, mods, o_sb, u, gvn, gate_w, gate_b, gn_sb, gn_gm, w_out_b, g_ff, w1_b, w2_b,
               g_final, tm, sample, final):
    nb, t, _ = x.shape
    r = mods.shape[2]
    tok = lambda w: pl.BlockSpec((None, tm, w), lambda b, i: (b, i, 0))

    def const(shape):
        return pl.BlockSpec(shape, lambda b, i: (0,) * len(shape), pipeline_mode=pl.Buffered(1))

    n_out = 2 if final else 1
    outs = pl.pallas_call(
        functools.partial(_post_kernel, sample=sample, final=final),
        grid=(nb, t // tm),
        in_specs=[
            tok(D_MODEL),
            pl.BlockSpec((None, 6, r, D_MODEL), lambda b, i: (b, 0, 0, 0)),
            tok(SB_WIDTH), tok(GM_WIDTH), tok(GM_WIDTH),
            const(gate_w.shape), const(gate_b.shape),
            const((1, SB_WIDTH)), const((1, GM_WIDTH)),
            const((SB_WIDTH + GM_WIDTH, D_MODEL)),
            const((1, D_MODEL)),
            const((D_MODEL, D_FF)), const((D_FF, D_MODEL)),
            const((1, D_MODEL)),
        ],
        out_specs=[tok(D_MODEL)] * n_out,
        out_shape=[jax.ShapeDtypeStruct((nb, t, D_MODEL), F32)] * n_out,
        compiler_params=pltpu.CompilerParams(
            dimension_semantics=("arbitrary", "arbitrary"), vmem_limit_bytes=VMEM_LIMIT),
        name="post_sample" if sample else "post_prompt",
    )(x, mods, o_sb, u, gvn, gate_w, gate_b, gn_sb.reshape(1, SB_WIDTH), gn_gm.reshape(1, GM_WIDTH),
      w_out_b, g_ff.reshape(1, D_MODEL), w1_b, w2_b, g_final.reshape(1, D_MODEL))
    return outs


def _suffix_sum_matrix():
    j = jnp.arange(LANES)[:, None]
    s = jnp.arange(LANES)[None, :]
    half = jnp.concatenate([(j >= s).astype(BF16), jnp.ones((LANES, LANES), BF16)], axis=1)
    return jnp.concatenate([half, half], axis=0)


def _sample_gate_pattern(w_s, b_s):
    t = jnp.arange(8) % DEC_SEQ
    pats = []
    for k in range(DEC_SEQ):
        w = jnp.where((t >= k)[None, :], w_s[:, t, jnp.maximum(t - k, 0)], 0.0)
        pats.append(jnp.repeat(w.T, HEAD_DIM, axis=1))
    pats.append(jnp.repeat(b_s[:, t].T, HEAD_DIM, axis=1))
    return jnp.stack(pats)


def kernel(x_prompt, x_sample, cache_k, cache_v, page_table, c_prompt, c_sample, w_ada, b_ada, g_mix,
           w_in, sb_bias, gm_vnorm, w_s, b_s, gn_sb, gn_gm, w_out, g_ff, w_ff1, w_ff2, g_final):
    depth = w_in.shape[0]
    n_b, seq, _ = x_prompt.shape
    n_seq, dec_seq, _ = x_sample.shape
    n_pool = cache_k.shape[1]
    n_tok_s = n_seq * dec_seq

    mods = _ada_call(jnp.concatenate([c_prompt, c_sample], axis=0), w_ada, b_ada)
    mods = mods.reshape(depth, n_b + n_seq, 6, D_MODEL)
    uu = _suffix_sum_matrix()
    cache_k = jnp.transpose(cache_k, (0, 1, 3, 4, 2)).reshape(depth, n_pool, SB_WIDTH, CHUNK)
    cache_v = jnp.transpose(cache_v, (0, 1, 3, 4, 2)).reshape(depth, n_pool, SB_WIDTH, CHUNK)

    xp = x_prompt
    xs = x_sample.reshape(1, n_tok_s, D_MODEL)
    kp, vp, gp, ksm, vsm, gsm = [], [], [], [], [], []
    yp = ys = None
    for l in range(depth):
        final = l == depth - 1
        w_in_b = w_in[l].astype(BF16)
        w_out_b = w_out[l].astype(BF16)
        w1_b = w_ff1[l].astype(BF16)
        w2_b = w_ff2[l].astype(BF16)
        mods_p = mods[l, :n_b][:, :, None, :]
        mods_s = jnp.repeat(jnp.swapaxes(mods[l, n_b:], 0, 1), dec_seq, axis=1)[None]
        bias_p = jnp.broadcast_to(
            jnp.repeat(sb_bias[l], CHUNK).reshape(N_PAIRS, 2 * CHUNK, 1), (N_PAIRS, 2 * CHUNK, LANES))
        bias_s = jnp.broadcast_to(jnp.tile(sb_bias[l], dec_seq)[:, None], (dec_seq * N_HEADS, LANES))
        gate_b_p = jnp.repeat(b_s[l].T, HEAD_DIM, axis=1)
        gate_pat_s = _sample_gate_pattern(w_s[l], b_s[l])

        qb, k, v, kb, vb, u, gvn = _pre_call(xp, mods_p, g_mix[l], w_in_b, gm_vnorm[l], 512)
        o_sb = _attn_prompt_call(qb, kb, vb, bias_p, uu)
        outs = _post_call(xp, mods_p, o_sb, u, gvn, w_s[l], gate_b_p, gn_sb[l], gn_gm[l], w_out_b,
                          g_ff[l], w1_b, w2_b, g_final, 512, False, final)
        xp = outs[0]
        if final:
            yp = outs[1]
        kp.append(k.reshape(n_b, seq // CHUNK, CHUNK, N_HEADS, HEAD_DIM))
        vp.append(v.reshape(n_b, seq // CHUNK, CHUNK, N_HEADS, HEAD_DIM))
        gp.append(gvn[:, seq - CHUNK:].reshape(n_b, CHUNK, N_HEADS, HEAD_DIM))

        qb, k, v, _, _, u, gvn = _pre_call(xs, mods_s, g_mix[l], w_in_b, gm_vnorm[l], n_tok_s)
        o_sb = _attn_sample_call(page_table, qb.reshape(n_seq, dec_seq, SB_WIDTH),
                                 k.reshape(n_seq, dec_seq, SB_WIDTH), v.reshape(n_seq, dec_seq, SB_WIDTH),
                                 cache_k, cache_v, l, bias_s, uu)
        outs = _post_call(xs, mods_s, o_sb.reshape(1, n_tok_s, SB_WIDTH), u, gvn, gate_pat_s,
                          gate_b_p, gn_sb[l], gn_gm[l], w_out_b, g_ff[l], w1_b, w2_b, g_final,
                          n_tok_s, True, final)
        xs = outs[0]
        if final:
            ys = outs[1]
        ksm.append(k.reshape(n_seq, dec_seq, N_HEADS, HEAD_DIM))
        vsm.append(v.reshape(n_seq, dec_seq, N_HEADS, HEAD_DIM))
        gsm.append(gvn.reshape(n_seq, dec_seq, N_HEADS, HEAD_DIM))

    return (yp, ys.reshape(n_seq, dec_seq, D_MODEL), jnp.stack(kp), jnp.stack(vp), jnp.stack(gp),
            jnp.stack(ksm), jnp.stack(vsm), jnp.stack(gsm))
```

```python
import functools

import jax
import jax.numpy as jnp
from jax import lax
from jax.experimental import pallas as pl
from jax.experimental.pallas import tpu as pltpu

D_MODEL = 1024
SB_WIDTH = 512
GM_WIDTH = 512
HEAD_DIM = 64
N_HEADS = 8
N_PAIRS = 4
IN_WIDTH = 3 * SB_WIDTH + 2 * GM_WIDTH
D_FF = 4 * D_MODEL
CHUNK = 128
DEC_SEQ = 4
EPS = 1e-6
LOG2_E = 1.4426950408889634
MASKED = -1e30
LANES = 128
VMEM_LIMIT = 56 * 1024 * 1024

F32 = jnp.float32
BF16 = jnp.bfloat16


def _rms(x, g):
    ms = jnp.mean(x * x, axis=-1, keepdims=True)
    return x * lax.rsqrt(ms + EPS) * g


def _softplus2(z):
    return jnp.maximum(z, 0.0) + jnp.log(1.0 + jnp.exp2(-jnp.abs(z))) * LOG2_E


def _split_bf16(x):
    hi = x.astype(BF16)
    lo = (x - hi.astype(F32)).astype(BF16)
    return jnp.concatenate([hi, lo], axis=1)


def _dot(a, b):
    return jnp.dot(a, b, preferred_element_type=F32)


def _dot_nt(a, b):
    return lax.dot_general(a, b, (((1,), (1,)), ((), ())), preferred_element_type=F32)


def _ada_kernel(c_ref, w_ref, b_ref, o_ref):
    c = c_ref[...]
    s = c * jax.nn.sigmoid(c)
    o_ref[0] = _dot(s.astype(BF16), w_ref[0].astype(BF16)) + b_ref[0]


def _ada_call(c_all, w_ada, b_ada):
    depth, _, n_out = w_ada.shape
    rows = c_all.shape[0]
    tn = 1024
    return pl.pallas_call(
        _ada_kernel,
        grid=(depth, n_out // tn),
        in_specs=[
            pl.BlockSpec((rows, D_MODEL), lambda l, j: (0, 0)),
            pl.BlockSpec((1, D_MODEL, tn), lambda l, j: (l, 0, j)),
            pl.BlockSpec((1, 1, tn), lambda l, j: (l, 0, j)),
        ],
        out_specs=pl.BlockSpec((1, rows, tn), lambda l, j: (l, 0, j)),
        out_shape=jax.ShapeDtypeStruct((depth, rows, n_out), F32),
        compiler_params=pltpu.CompilerParams(
            dimension_semantics=("arbitrary", "arbitrary"), vmem_limit_bytes=VMEM_LIMIT),
        name="ada",
    )(c_all, w_ada, b_ada.reshape(depth, 1, n_out))


def _pre_kernel(x_ref, m_ref, gmix_ref, wa_ref, wkv_ref, gvnorm_ref, *refs, pages):
    q_ref, u_ref, gvn_ref, k_ref, v_ref, kb_ref, vb_ref = refs[-7:]
    x = x_ref[...]
    h = (_rms(x, gmix_ref[...]) * (1.0 + m_ref[1]) + m_ref[0]).astype(BF16)
    pa = _dot(h, wa_ref[...])
    q_ref[...] = (pa[:, :SB_WIDTH] * (HEAD_DIM ** -0.5 * LOG2_E)).astype(BF16)
    u_ref[...] = pa[:, SB_WIDTH:SB_WIDTH + GM_WIDTH]
    gvn_ref[...] = _rms(pa[:, SB_WIDTH + GM_WIDTH:], gvnorm_ref[...])
    if pages:
        kv = _dot_nt(wkv_ref[...], h)
        for c in range(pages):
            blk = kv[:, c * CHUNK:(c + 1) * CHUNK]
            k_ref[c] = blk[:SB_WIDTH]
            v_ref[c] = blk[SB_WIDTH:]
            kb_ref[c] = blk[:SB_WIDTH].astype(BF16)
            vb_ref[c] = blk[SB_WIDTH:].astype(BF16)
    else:
        kv = _dot(h, wkv_ref[...])
        k_ref[...] = kv[:, :SB_WIDTH]
        v_ref[...] = kv[:, SB_WIDTH:]
        kb_ref[...] = kv[:, :SB_WIDTH].astype(BF16)
        vb_ref[...] = kv[:, SB_WIDTH:].astype(BF16)


def _pre_call(x, mods, g_mix, w_a, w_kv, gm_vnorm, tm, paged=None):
    nb, t, _ = x.shape
    r = mods.shape[2]
    tok = lambda w: pl.BlockSpec((None, tm, w), lambda b, i: (b, i, 0))
    const = lambda shape: pl.BlockSpec(shape, lambda b, i: (0,) * len(shape))
    out_specs = [tok(SB_WIDTH), tok(GM_WIDTH), tok(GM_WIDTH)]
    out_shape = [jax.ShapeDtypeStruct((nb, t, SB_WIDTH), BF16),
                 jax.ShapeDtypeStruct((nb, t, GM_WIDTH), F32),
                 jax.ShapeDtypeStruct((nb, t, GM_WIDTH), F32)]
    operands = [x, mods, g_mix.reshape(1, D_MODEL), w_a, w_kv, gm_vnorm.reshape(1, GM_WIDTH)]
    in_specs = [tok(D_MODEL),
                pl.BlockSpec((None, 6, r, D_MODEL), lambda b, i: (b, 0, 0, 0)),
                const((1, D_MODEL)), const(w_a.shape), const(w_kv.shape), const((1, GM_WIDTH))]
    aliases = {}
    if paged is None:
        pages = 0
        out_specs += [tok(SB_WIDTH)] * 4
        out_shape += [jax.ShapeDtypeStruct((nb, t, SB_WIDTH), dt) for dt in (F32, F32, BF16, BF16)]
    else:
        layer, depth, k_all, v_all = paged
        pages = tm // CHUNK
        n_pages = t // CHUNK
        page_shape = (pages, SB_WIDTH, CHUNK)
        out_specs += [pl.BlockSpec((None, None) + page_shape, lambda b, i: (layer, b, i, 0, 0))] * 2
        out_specs += [pl.BlockSpec((None,) + page_shape, lambda b, i: (b, i, 0, 0))] * 2
        out_shape += [jax.ShapeDtypeStruct((depth, nb, n_pages, SB_WIDTH, CHUNK), F32)] * 2
        out_shape += [jax.ShapeDtypeStruct((nb, n_pages, SB_WIDTH, CHUNK), BF16)] * 2
        if k_all is not None:
            aliases = {len(operands): 3, len(operands) + 1: 4}
            operands += [k_all, v_all]
            in_specs += [pl.BlockSpec(memory_space=pl.ANY)] * 2
    return pl.pallas_call(
        functools.partial(_pre_kernel, pages=pages),
        grid=(nb, t // tm),
        in_specs=in_specs,
        out_specs=out_specs,
        out_shape=out_shape,
        input_output_aliases=aliases,
        compiler_params=pltpu.CompilerParams(
            dimension_semantics=("arbitrary", "arbitrary"), vmem_limit_bytes=VMEM_LIMIT),
        name="pre",
    )(*operands)


def _attn_prompt_kernel(q_ref, k_ref, v_ref, bias_ref, uu_ref, o_ref,
                        q2_ref, z_ref, sp_ref, acc_ref, carry_ref):
    qi = pl.program_id(1)
    lane = lax.broadcasted_iota(jnp.int32, (CHUNK, LANES), 1)
    row = lax.broadcasted_iota(jnp.int32, (CHUNK, LANES), 0)
    first_head = lane < HEAD_DIM
    causal = lane < row
    causal2 = jnp.concatenate([causal, causal], axis=0)
    cols = [slice(hp * LANES, (hp + 1) * LANES) for hp in range(N_PAIRS)]

    qf = q_ref[...].astype(F32)
    for hp in range(N_PAIRS):
        qp = qf[:, cols[hp]]
        q2_ref[hp] = jnp.concatenate(
            [jnp.where(first_head, qp, 0.0), jnp.where(first_head, 0.0, qp)], axis=0).astype(BF16)
        acc_ref[hp] = jnp.zeros((2 * CHUNK, LANES), F32)
        carry_ref[hp] = jnp.zeros((2 * CHUNK, LANES), F32)

    def scores(kb):
        ks = k_ref[kb]
        return [_dot(q2_ref[hp], ks[cols[hp], :]) + bias_ref[hp] for hp in range(N_PAIRS)]

    def stash(zs):
        for hp in range(N_PAIRS):
            z_ref[hp] = zs[hp]
            sp_ref[hp] = _split_bf16(_softplus2(zs[hp]))

    def finish(kb, score_next):
        vs = v_ref[kb]
        zs = [z_ref[hp] for hp in range(N_PAIRS)]
        rs = [_dot(sp_ref[hp], uu_ref[...]) for hp in range(N_PAIRS)]
        zs_next = scores(kb - 1) if score_next else None
        es = []
        for hp in range(N_PAIRS):
            carry = carry_ref[hp]
            es.append(jnp.exp2(zs[hp] - (rs[hp][:, :LANES] + carry)).astype(BF16))
            carry_ref[hp] = carry + rs[hp][:, LANES:]
        for hp in range(N_PAIRS):
            acc_ref[hp] += _dot_nt(es[hp], vs[cols[hp], :])
        if score_next:
            stash(zs_next)

    stash([jnp.where(causal2, z, MASKED) for z in scores(qi)])

    def body(i, c):
        finish(qi - i, True)
        return c

    lax.fori_loop(0, qi, body, 0)
    finish(0, False)

    for hp in range(N_PAIRS):
        a = acc_ref[hp]
        o_ref[:, cols[hp]] = jnp.where(first_head, a[:CHUNK], a[CHUNK:])


def _attn_prompt_call(qb, kb, vb, bias_p, uu):
    nb, t, _ = qb.shape
    stacked = (N_PAIRS, 2 * CHUNK, LANES)
    return pl.pallas_call(
        _attn_prompt_kernel,
        grid=(nb, t // CHUNK),
        in_specs=[
            pl.BlockSpec((None, CHUNK, SB_WIDTH), lambda b, i: (b, i, 0)),
            pl.BlockSpec((None,) + kb.shape[1:], lambda b, i: (b, 0, 0, 0)),
            pl.BlockSpec((None,) + vb.shape[1:], lambda b, i: (b, 0, 0, 0)),
            pl.BlockSpec(stacked, lambda b, i: (0, 0, 0)),
            pl.BlockSpec((2 * LANES, 2 * LANES), lambda b, i: (0, 0)),
        ],
        out_specs=pl.BlockSpec((None, CHUNK, SB_WIDTH), lambda b, i: (b, i, 0)),
        out_shape=jax.ShapeDtypeStruct((nb, t, SB_WIDTH), F32),
        scratch_shapes=[
            pltpu.VMEM(stacked, BF16),
            pltpu.VMEM(stacked, F32),
            pltpu.VMEM((N_PAIRS, 2 * CHUNK, 2 * LANES), BF16),
            pltpu.VMEM(stacked, F32),
            pltpu.VMEM(stacked, F32),
        ],
        compiler_params=pltpu.CompilerParams(
            dimension_semantics=("arbitrary", "arbitrary"), vmem_limit_bytes=VMEM_LIMIT),
        name="attn_prompt",
    )(qb, kb, vb, bias_p, uu)


def _attn_sample_kernel(pt_ref, q_ref, kn_ref, vn_ref, *refs, n_pages):
    k_refs, v_refs = refs[:n_pages], refs[n_pages:2 * n_pages]
    bias_ref, uu_ref, o_ref, kpad_ref, vpad_ref = refs[2 * n_pages:]
    rows = DEC_SEQ * N_HEADS

    lane = lax.broadcasted_iota(jnp.int32, (rows, SB_WIDTH), 1)
    row = lax.broadcasted_iota(jnp.int32, (rows, SB_WIDTH), 0)
    own_head = lane // HEAD_DIM == row % N_HEADS
    qf = q_ref[...].astype(F32)
    qrep = jnp.broadcast_to(qf[:, None, :], (DEC_SEQ, N_HEADS, SB_WIDTH)).reshape(rows, SB_WIDTH)
    q2 = jnp.where(own_head, qrep, 0.0).astype(BF16)

    kpad_ref[...] = jnp.zeros_like(kpad_ref)
    vpad_ref[...] = jnp.zeros_like(vpad_ref)
    kpad_ref[0:DEC_SEQ, :] = kn_ref[...]
    vpad_ref[0:DEC_SEQ, :] = vn_ref[...]
    col = lax.broadcasted_iota(jnp.int32, (rows, LANES), 1)
    qpos = lax.broadcasted_iota(jnp.int32, (rows, LANES), 0) // N_HEADS
    bias = bias_ref[...]

    zs = [jnp.where(col < qpos, _dot_nt(q2, kpad_ref[...].astype(BF16)) + bias, MASKED)]
    for p in reversed(range(n_pages)):
        zs.append(_dot(q2, k_refs[p][...].astype(BF16)) + bias)
    sp = jnp.concatenate([_split_bf16(_softplus2(z)) for z in zs], axis=0)
    r = _dot(sp, uu_ref[...])

    carry = jnp.zeros((rows, LANES), F32)
    acc = jnp.zeros((rows, SB_WIDTH), F32)
    for i, z in enumerate(zs):
        ri = r[i * rows:(i + 1) * rows]
        e = jnp.exp2(z - (ri[:, :LANES] + carry)).astype(BF16)
        carry = carry + ri[:, LANES:]
        if i == 0:
            acc = acc + _dot(e, vpad_ref[...].astype(BF16))
        else:
            acc = acc + _dot_nt(e, v_refs[n_pages - i][...].astype(BF16))

    own = jnp.where(own_head, acc, 0.0)
    o_ref[...] = jnp.sum(own.reshape(DEC_SEQ, N_HEADS, SB_WIDTH), axis=1)


def _attn_sample_call(page_table, qb, k_new, v_new, cache_k, cache_v, layer, bias_s, uu):
    n_seq, n_pages = page_table.shape
    rows = DEC_SEQ * N_HEADS
    new = lambda: pl.BlockSpec((None, DEC_SEQ, SB_WIDTH), lambda b, pt: (b, 0, 0))

    def page(p):
        return pl.BlockSpec((None, None, SB_WIDTH, CHUNK), lambda b, pt: (layer, pt[b, p], 0, 0))

    grid_spec = pltpu.PrefetchScalarGridSpec(
        num_scalar_prefetch=1,
        grid=(n_seq,),
        in_specs=[new(), new(), new()] + [page(p) for p in range(n_pages)] * 2 + [
            pl.BlockSpec((rows, LANES), lambda b, pt: (0, 0)),
            pl.BlockSpec((2 * LANES, 2 * LANES), lambda b, pt: (0, 0)),
        ],
        out_specs=pl.BlockSpec((None, DEC_SEQ, SB_WIDTH), lambda b, pt: (b, 0, 0)),
        scratch_shapes=[
            pltpu.VMEM((CHUNK, SB_WIDTH), F32),
            pltpu.VMEM((CHUNK, SB_WIDTH), F32),
        ],
    )
    return pl.pallas_call(
        functools.partial(_attn_sample_kernel, n_pages=n_pages),
        grid_spec=grid_spec,
        out_shape=jax.ShapeDtypeStruct((n_seq, DEC_SEQ, SB_WIDTH), F32),
        compiler_params=pltpu.CompilerParams(
            dimension_semantics=("arbitrary",), vmem_limit_bytes=VMEM_LIMIT),
        name="attn_sample",
    )(page_table, qb, k_new, v_new, *([cache_k] * n_pages), *([cache_v] * n_pages), bias_s, uu)


def _gate_prompt(gvn, ws_ref, gbias):
    tm = gvn.shape[0]
    lane = lax.broadcasted_iota(jnp.int32, (CHUNK, LANES), 1)
    row = lax.broadcasted_iota(jnp.int32, (CHUNK, LANES), 0)
    first_group = lane < HEAD_DIM
    keep = lane <= row
    out_cols = []
    for gp in range(N_PAIRS):
        w_pair = jnp.concatenate(
            [jnp.where(keep, ws_ref[2 * gp], 0.0), jnp.where(keep, ws_ref[2 * gp + 1], 0.0)],
            axis=1).astype(BF16)
        chunks = []
        for c in range(tm // CHUNK):
            r = gvn[c * CHUNK:(c + 1) * CHUNK, gp * LANES:(gp + 1) * LANES]
            r2 = jnp.concatenate(
                [jnp.where(first_group, r, 0.0), jnp.where(first_group, 0.0, r)], axis=0).astype(BF16)
            chunks.append(_dot(w_pair, r2) + gbias[:, gp * LANES:(gp + 1) * LANES])
        out_cols.append(jnp.concatenate(chunks, axis=0))
    return jnp.concatenate(out_cols, axis=1)


def _gate_sample(gvn, pat_ref):
    tm = gvn.shape[0]
    g3 = gvn.reshape(tm // 8, 8, GM_WIDTH)
    out = g3 * pat_ref[0][None] + pat_ref[DEC_SEQ][None]
    for k in range(1, DEC_SEQ):
        out = out + pltpu.roll(g3, k, axis=1) * pat_ref[k][None]
    return out.reshape(tm, GM_WIDTH)


def _post_kernel(x_ref, m_ref, osb_ref, u_ref, gvn_ref, gate_w_ref, gate_b_ref, gnsb_ref, gngm_ref,
                 wout_ref, gff_ref, w1_ref, w2_ref, gfin_ref, *out_refs, sample, final):
    x = x_ref[...]
    o_sb = _rms(osb_ref[...], gnsb_ref[...])
    if sample:
        mixed = _gate_sample(gvn_ref[...], gate_w_ref)
    else:
        mixed = _gate_prompt(gvn_ref[...], gate_w_ref, gate_b_ref[...])
    o_gm = _rms(u_ref[...] * mixed, gngm_ref[...])
    cat = jnp.concatenate([o_sb, o_gm], axis=-1).astype(BF16)
    x = x + m_ref[2] * _dot(cat, wout_ref[...])
    h = (_rms(x, gff_ref[...]) * (1.0 + m_ref[4]) + m_ref[3]).astype(BF16)
    ff = None
    n_split = 4
    cw = D_FF // n_split
    for c in range(n_split):
        a = jnp.maximum(_dot(h, w1_ref[:, c * cw:(c + 1) * cw]), 0.0)
        part = _dot((a * a).astype(BF16), w2_ref[c * cw:(c + 1) * cw, :])
        ff = part if ff is None else ff + part
    x = x + m_ref[5] * ff
    out_refs[0][...] = x
    if final:
        out_refs[1][...] = _rms(x, gfin_ref[...])


def _post_call(x, mods, o_sb, u, gvn, gate_w, gate_b, gn_sb, gn_gm, w_out_b, g_ff, w1_b, w2_b,
               g_final, tm, sample, final):
    nb, t, _ = x.shape
    r = mods.shape[2]
    tok = lambda w: pl.BlockSpec((None, tm, w), lambda b, i: (b, i, 0))

    def const(shape):
        return pl.BlockSpec(shape, lambda b, i: (0,) * len(shape), pipeline_mode=pl.Buffered(1))

    n_out = 2 if final else 1
    outs = pl.pallas_call(
        functools.partial(_post_kernel, sample=sample, final=final),
        grid=(nb, t // tm),
        in_specs=[
            tok(D_MODEL),
            pl.BlockSpec((None, 6, r, D_MODEL), lambda b, i: (b, 0, 0, 0)),
            tok(SB_WIDTH), tok(GM_WIDTH), tok(GM_WIDTH),
            const(gate_w.shape), const(gate_b.shape),
            const((1, SB_WIDTH)), const((1, GM_WIDTH)),
            const((SB_WIDTH + GM_WIDTH, D_MODEL)),
            const((1, D_MODEL)),
            const((D_MODEL, D_FF)), const((D_FF, D_MODEL)),
            const((1, D_MODEL)),
        ],
        out_specs=[tok(D_MODEL)] * n_out,
        out_shape=[jax.ShapeDtypeStruct((nb, t, D_MODEL), F32)] * n_out,
        compiler_params=pltpu.CompilerParams(
            dimension_semantics=("arbitrary", "arbitrary"), vmem_limit_bytes=VMEM_LIMIT),
        name="post_sample" if sample else "post_prompt",
    )(x, mods, o_sb, u, gvn, gate_w, gate_b, gn_sb.reshape(1, SB_WIDTH), gn_gm.reshape(1, GM_WIDTH),
      w_out_b, g_ff.reshape(1, D_MODEL), w1_b, w2_b, g_final.reshape(1, D_MODEL))
    return outs


def _suffix_sum_matrix():
    j = jnp.arange(LANES)[:, None]
    s = jnp.arange(LANES)[None, :]
    half = jnp.concatenate([(j >= s).astype(BF16), jnp.ones((LANES, LANES), BF16)], axis=1)
    return jnp.concatenate([half, half], axis=0)


def _sample_gate_pattern(w_s, b_s):
    zero = jnp.zeros((w_s.shape[0],), F32)
    pats = []
    for k in range(DEC_SEQ):
        rows = [w_s[:, r % DEC_SEQ, r % DEC_SEQ - k] if r % DEC_SEQ >= k else zero for r in range(8)]
        pats.append(jnp.repeat(jnp.stack(rows), HEAD_DIM, axis=1))
    pats.append(jnp.repeat(jnp.stack([b_s[:, r % DEC_SEQ] for r in range(8)]), HEAD_DIM, axis=1))
    return jnp.stack(pats)


def kernel(x_prompt, x_sample, cache_k, cache_v, page_table, c_prompt, c_sample, w_ada, b_ada, g_mix,
           w_in, sb_bias, gm_vnorm, w_s, b_s, gn_sb, gn_gm, w_out, g_ff, w_ff1, w_ff2, g_final):
    depth = w_in.shape[0]
    n_b, seq, _ = x_prompt.shape
    n_seq, dec_seq, _ = x_sample.shape
    n_pool = cache_k.shape[1]
    n_tok_s = n_seq * dec_seq

    mods = _ada_call(jnp.concatenate([c_prompt, c_sample], axis=0), w_ada, b_ada)
    mods = mods.reshape(depth, n_b + n_seq, 6, D_MODEL)
    uu = _suffix_sum_matrix()
    cache_k = jnp.transpose(cache_k, (0, 1, 3, 4, 2)).reshape(depth, n_pool, SB_WIDTH, CHUNK)
    cache_v = jnp.transpose(cache_v, (0, 1, 3, 4, 2)).reshape(depth, n_pool, SB_WIDTH, CHUNK)

    xp = x_prompt
    xs = x_sample.reshape(1, n_tok_s, D_MODEL)
    gp, ksm, vsm, gsm = [], [], [], []
    yp = ys = k_pages = v_pages = None
    for l in range(depth):
        final = l == depth - 1
        w_a = jnp.concatenate([w_in[l][:, :SB_WIDTH], w_in[l][:, 3 * SB_WIDTH:]], axis=1).astype(BF16)
        w_kv = w_in[l][:, SB_WIDTH:3 * SB_WIDTH].astype(BF16)
        w_out_b = w_out[l].astype(BF16)
        w1_b = w_ff1[l].astype(BF16)
        w2_b = w_ff2[l].astype(BF16)
        mods_p = mods[l, :n_b][:, :, None, :]
        mods_s = jnp.repeat(jnp.swapaxes(mods[l, n_b:], 0, 1), dec_seq, axis=1)[None]
        bias2 = sb_bias[l] * LOG2_E
        bias_p = jnp.broadcast_to(
            jnp.repeat(bias2, CHUNK).reshape(N_PAIRS, 2 * CHUNK, 1), (N_PAIRS, 2 * CHUNK, LANES))
        bias_s = jnp.broadcast_to(jnp.tile(bias2, dec_seq)[:, None], (dec_seq * N_HEADS, LANES))
        gate_b_p = jnp.repeat(b_s[l].T, HEAD_DIM, axis=1)
        gate_pat_s = _sample_gate_pattern(w_s[l], b_s[l])

        qb, u, gvn, k_pages, v_pages, kb, vb = _pre_call(
            xp, mods_p, g_mix[l], w_a, w_kv.T, gm_vnorm[l], 512, paged=(l, depth, k_pages, v_pages))
        o_sb = _attn_prompt_call(qb, kb, vb, bias_p, uu)
        outs = _post_call(xp, mods_p, o_sb, u, gvn, w_s[l], gate_b_p, gn_sb[l], gn_gm[l], w_out_b,
                          g_ff[l], w1_b, w2_b, g_final, 512, False, final)
        xp = outs[0]
        if final:
            yp = outs[1]
        gp.append(gvn[:, seq - CHUNK:].reshape(n_b, CHUNK, N_HEADS, HEAD_DIM))

        qb, u, gvn, k, v, _, _ = _pre_call(xs, mods_s, g_mix[l], w_a, w_kv, gm_vnorm[l], n_tok_s)
        o_sb = _attn_sample_call(page_table, qb.reshape(n_seq, dec_seq, SB_WIDTH),
                                 k.reshape(n_seq, dec_seq, SB_WIDTH), v.reshape(n_seq, dec_seq, SB_WIDTH),
                                 cache_k, cache_v, l, bias_s, uu)
        outs = _post_call(xs, mods_s, o_sb.reshape(1, n_tok_s, SB_WIDTH), u, gvn, gate_pat_s,
                          gate_b_p, gn_sb[l], gn_gm[l], w_out_b, g_ff[l], w1_b, w2_b, g_final,
                          n_tok_s, True, final)
        xs = outs[0]
        if final:
            ys = outs[1]
        ksm.append(k.reshape(n_seq, dec_seq, N_HEADS, HEAD_DIM))
        vsm.append(v.reshape(n_seq, dec_seq, N_HEADS, HEAD_DIM))
        gsm.append(gvn.reshape(n_seq, dec_seq, N_HEADS, HEAD_DIM))

    def unpage(pages):
        p = pages.reshape(depth, n_b, seq // CHUNK, N_HEADS, HEAD_DIM, CHUNK)
        return jnp.transpose(p, (0, 1, 2, 5, 3, 4))

    return (yp, ys.reshape(n_seq, dec_seq, D_MODEL), unpage(k_pages), unpage(v_pages), jnp.stack(gp),
            jnp.stack(ksm), jnp.stack(vsm), jnp.stack(gsm))
```

```python
import functools

import jax
import jax.numpy as jnp
from jax import lax
from jax.experimental import pallas as pl
from jax.experimental.pallas import tpu as pltpu

D_MODEL = 1024
SB_WIDTH = 512
GM_WIDTH = 512
HEAD_DIM = 64
N_HEADS = 8
N_PAIRS = 4
IN_WIDTH = 3 * SB_WIDTH + 2 * GM_WIDTH
D_FF = 4 * D_MODEL
CHUNK = 128
DEC_SEQ = 4
EPS = 1e-6
LOG2_E = 1.4426950408889634
MASKED = -1e30
LANES = 128
UNROLLS = (4, 2, 1)
VMEM_LIMIT = 56 * 1024 * 1024

F32 = jnp.float32
BF16 = jnp.bfloat16


def _rms(x, g):
    ms = jnp.mean(x * x, axis=-1, keepdims=True)
    return x * lax.rsqrt(ms + EPS) * g


def _softplus2(z):
    return jnp.maximum(z, 0.0) + jnp.log(1.0 + jnp.exp2(-jnp.abs(z))) * LOG2_E


def _split_bf16(x):
    hi = x.astype(BF16)
    lo = (x - hi.astype(F32)).astype(BF16)
    return jnp.concatenate([hi, lo], axis=1)


def _dot(a, b):
    return jnp.dot(a, b, preferred_element_type=F32)


def _dot_nt(a, b):
    return lax.dot_general(a, b, (((1,), (1,)), ((), ())), preferred_element_type=F32)


def _ada_kernel(c_ref, w_ref, b_ref, o_ref):
    c = c_ref[...]
    s = c * jax.nn.sigmoid(c)
    o_ref[0] = _dot(s.astype(BF16), w_ref[0].astype(BF16)) + b_ref[0]


def _ada_call(c_all, w_ada, b_ada):
    depth, _, n_out = w_ada.shape
    rows = c_all.shape[0]
    tn = 1024
    return pl.pallas_call(
        _ada_kernel,
        grid=(depth, n_out // tn),
        in_specs=[
            pl.BlockSpec((rows, D_MODEL), lambda l, j: (0, 0)),
            pl.BlockSpec((1, D_MODEL, tn), lambda l, j: (l, 0, j)),
            pl.BlockSpec((1, 1, tn), lambda l, j: (l, 0, j)),
        ],
        out_specs=pl.BlockSpec((1, rows, tn), lambda l, j: (l, 0, j)),
        out_shape=jax.ShapeDtypeStruct((depth, rows, n_out), F32),
        compiler_params=pltpu.CompilerParams(
            dimension_semantics=("arbitrary", "arbitrary"), vmem_limit_bytes=VMEM_LIMIT),
        name="ada",
    )(c_all, w_ada, b_ada.reshape(depth, 1, n_out))


def _pre_kernel(x_ref, m_ref, gmix_ref, wa_ref, wkv_ref, gvnorm_ref, *refs, pages):
    q_ref, u_ref, gvn_ref, k_ref, v_ref, kb_ref, vb_ref = refs[-7:]
    x = x_ref[...]
    shift, scale = m_ref[:, :D_MODEL], m_ref[:, D_MODEL:]
    h = (_rms(x, gmix_ref[...]) * (1.0 + scale) + shift).astype(BF16)
    pa = _dot(h, wa_ref[...])
    q_ref[...] = (pa[:, :SB_WIDTH] * (HEAD_DIM ** -0.5 * LOG2_E)).astype(BF16)
    u_ref[...] = pa[:, SB_WIDTH:SB_WIDTH + GM_WIDTH]
    gvn_ref[...] = _rms(pa[:, SB_WIDTH + GM_WIDTH:], gvnorm_ref[...])
    if pages:
        kv = _dot_nt(wkv_ref[...], h)
        for c in range(pages):
            blk = kv[:, c * CHUNK:(c + 1) * CHUNK]
            k_ref[c] = blk[:SB_WIDTH]
            v_ref[c] = blk[SB_WIDTH:]
            kb_ref[c] = blk[:SB_WIDTH].astype(BF16)
            vb_ref[c] = blk[SB_WIDTH:].astype(BF16)
    else:
        kv = _dot(h, wkv_ref[...])
        k_ref[...] = kv[:, :SB_WIDTH]
        v_ref[...] = kv[:, SB_WIDTH:]
        kb_ref[...] = kv[:, :SB_WIDTH].astype(BF16)
        vb_ref[...] = kv[:, SB_WIDTH:].astype(BF16)


def _mod_spec(rows, pair, lead):
    if lead is None:
        return pl.BlockSpec((None, rows, 2 * D_MODEL), lambda b, i: (b, 0, pair))
    return pl.BlockSpec((None, rows, 2 * D_MODEL), lambda b, i: (lead, 0, pair),
                        pipeline_mode=pl.Buffered(1))


def _layer_spec(w, layer):
    return pl.BlockSpec((None,) + w.shape[1:], lambda b, i: (layer,) + (0,) * (w.ndim - 1),
                        pipeline_mode=pl.Buffered(1))


def _pre_call(x, mods, mod_lead, layer, g_mix, w_a, w_kv, gm_vnorm, tm, paged=None):
    nb, t, _ = x.shape
    tok = lambda w: pl.BlockSpec((None, tm, w), lambda b, i: (b, i, 0))
    const = lambda shape: pl.BlockSpec(shape, lambda b, i: (0,) * len(shape))
    out_specs = [tok(SB_WIDTH), tok(GM_WIDTH), tok(GM_WIDTH)]
    out_shape = [jax.ShapeDtypeStruct((nb, t, SB_WIDTH), BF16),
                 jax.ShapeDtypeStruct((nb, t, GM_WIDTH), F32),
                 jax.ShapeDtypeStruct((nb, t, GM_WIDTH), F32)]
    operands = [x, mods, g_mix.reshape(1, D_MODEL), w_a, w_kv, gm_vnorm.reshape(1, GM_WIDTH)]
    in_specs = [tok(D_MODEL), _mod_spec(1 if mod_lead is None else tm, 0, mod_lead),
                const((1, D_MODEL)), _layer_spec(w_a, layer), _layer_spec(w_kv, layer),
                const((1, GM_WIDTH))]
    aliases = {}
    if paged is None:
        pages = 0
        out_specs += [tok(SB_WIDTH)] * 4
        out_shape += [jax.ShapeDtypeStruct((nb, t, SB_WIDTH), dt) for dt in (F32, F32, BF16, BF16)]
    else:
        depth, k_all, v_all = paged
        pages = tm // CHUNK
        n_pages = t // CHUNK
        page_shape = (pages, SB_WIDTH, CHUNK)
        out_specs += [pl.BlockSpec((None, None) + page_shape, lambda b, i: (layer, b, i, 0, 0))] * 2
        out_specs += [pl.BlockSpec((None,) + page_shape, lambda b, i: (b, i, 0, 0))] * 2
        out_shape += [jax.ShapeDtypeStruct((depth, nb, n_pages, SB_WIDTH, CHUNK), F32)] * 2
        out_shape += [jax.ShapeDtypeStruct((nb, n_pages, SB_WIDTH, CHUNK), BF16)] * 2
        if k_all is not None:
            aliases = {len(operands): 3, len(operands) + 1: 4}
            operands += [k_all, v_all]
            in_specs += [pl.BlockSpec(memory_space=pl.ANY)] * 2
    return pl.pallas_call(
        functools.partial(_pre_kernel, pages=pages),
        grid=(nb, t // tm),
        in_specs=in_specs,
        out_specs=out_specs,
        out_shape=out_shape,
        input_output_aliases=aliases,
        compiler_params=pltpu.CompilerParams(
            dimension_semantics=("arbitrary", "arbitrary"), vmem_limit_bytes=VMEM_LIMIT),
        name="pre",
    )(*operands)


def _attn_prompt_kernel(q_ref, k_ref, v_ref, bias_ref, uu_ref, o_ref,
                        q2_ref, z_ref, sp_ref, acc_ref, carry_ref):
    qi = pl.program_id(1)
    lane = lax.broadcasted_iota(jnp.int32, (CHUNK, LANES), 1)
    row = lax.broadcasted_iota(jnp.int32, (CHUNK, LANES), 0)
    first_head = lane < HEAD_DIM
    causal = lane < row
    causal2 = jnp.concatenate([causal, causal], axis=0)
    cols = [slice(hp * LANES, (hp + 1) * LANES) for hp in range(N_PAIRS)]

    qf = q_ref[...].astype(F32)
    for hp in range(N_PAIRS):
        qp = qf[:, cols[hp]]
        q2_ref[hp] = jnp.concatenate(
            [jnp.where(first_head, qp, 0.0), jnp.where(first_head, 0.0, qp)], axis=0).astype(BF16)
        acc_ref[hp] = jnp.zeros((2 * CHUNK, LANES), F32)
        carry_ref[hp] = jnp.zeros((2 * CHUNK, LANES), F32)

    def scores(kb):
        ks = k_ref[kb]
        return [_dot(q2_ref[hp], ks[cols[hp], :]) + bias_ref[hp] for hp in range(N_PAIRS)]

    def stash(zs):
        for hp in range(N_PAIRS):
            z_ref[hp] = zs[hp]
            sp_ref[hp] = _softplus2(zs[hp]).astype(BF16)

    def finish(kb, score_next):
        vs = v_ref[kb]
        zs = [z_ref[hp] for hp in range(N_PAIRS)]
        rs = [_dot(sp_ref[hp], uu_ref[:LANES, :]) for hp in range(N_PAIRS)]
        zs_next = scores(kb - 1) if score_next else None
        es = []
        for hp in range(N_PAIRS):
            carry = carry_ref[hp]
            es.append(jnp.exp2(zs[hp] - (rs[hp][:, :LANES] + carry)).astype(BF16))
            carry_ref[hp] = carry + rs[hp][:, LANES:]
        for hp in range(N_PAIRS):
            acc_ref[hp] += _dot_nt(es[hp], vs[cols[hp], :])
        if score_next:
            stash(zs_next)

    stash([jnp.where(causal2, z, MASKED) for z in scores(qi)])

    top = qi
    for width in UNROLLS:
        def body(i, c, top=top, width=width):
            for j in range(width):
                finish(top - width * i - j, True)
            return c

        trips = top // width
        lax.fori_loop(0, trips, body, 0)
        top = top - trips * width
    finish(0, False)

    for hp in range(N_PAIRS):
        a = acc_ref[hp]
        o_ref[:, cols[hp]] = jnp.where(first_head, a[:CHUNK], a[CHUNK:])


def _attn_prompt_call(qb, kb, vb, bias_p, uu):
    nb, t, _ = qb.shape
    stacked = (N_PAIRS, 2 * CHUNK, LANES)
    return pl.pallas_call(
        _attn_prompt_kernel,
        grid=(nb, t // CHUNK),
        in_specs=[
            pl.BlockSpec((None, CHUNK, SB_WIDTH), lambda b, i: (b, i, 0)),
            pl.BlockSpec((None,) + kb.shape[1:], lambda b, i: (b, 0, 0, 0)),
            pl.BlockSpec((None,) + vb.shape[1:], lambda b, i: (b, 0, 0, 0)),
            pl.BlockSpec(stacked, lambda b, i: (0, 0, 0)),
            pl.BlockSpec((2 * LANES, 2 * LANES), lambda b, i: (0, 0)),
        ],
        out_specs=pl.BlockSpec((None, CHUNK, SB_WIDTH), lambda b, i: (b, i, 0)),
        out_shape=jax.ShapeDtypeStruct((nb, t, SB_WIDTH), F32),
        scratch_shapes=[
            pltpu.VMEM(stacked, BF16),
            pltpu.VMEM(stacked, F32),
            pltpu.VMEM(stacked, BF16),
            pltpu.VMEM(stacked, F32),
            pltpu.VMEM(stacked, F32),
        ],
        compiler_params=pltpu.CompilerParams(
            dimension_semantics=("arbitrary", "arbitrary"), vmem_limit_bytes=VMEM_LIMIT),
        name="attn_prompt",
    )(qb, kb, vb, bias_p, uu)


def _attn_sample_kernel(pt_ref, q_ref, kn_ref, vn_ref, *refs, n_pages):
    k_refs, v_refs = refs[:n_pages], refs[n_pages:2 * n_pages]
    bias_ref, uu_ref, o_ref, kpad_ref, vpad_ref = refs[2 * n_pages:]
    rows = DEC_SEQ * N_HEADS

    lane = lax.broadcasted_iota(jnp.int32, (rows, SB_WIDTH), 1)
    row = lax.broadcasted_iota(jnp.int32, (rows, SB_WIDTH), 0)
    own_head = lane // HEAD_DIM == row % N_HEADS
    qf = q_ref[...].astype(F32)
    qrep = jnp.broadcast_to(qf[:, None, :], (DEC_SEQ, N_HEADS, SB_WIDTH)).reshape(rows, SB_WIDTH)
    q2 = jnp.where(own_head, qrep, 0.0).astype(BF16)

    kpad_ref[...] = jnp.zeros_like(kpad_ref)
    vpad_ref[...] = jnp.zeros_like(vpad_ref)
    kpad_ref[0:DEC_SEQ, :] = kn_ref[...]
    vpad_ref[0:DEC_SEQ, :] = vn_ref[...]
    col = lax.broadcasted_iota(jnp.int32, (rows, LANES), 1)
    qpos = lax.broadcasted_iota(jnp.int32, (rows, LANES), 0) // N_HEADS
    bias = bias_ref[...]

    zs = [jnp.where(col < qpos, _dot_nt(q2, kpad_ref[...].astype(BF16)) + bias, MASKED)]
    for p in reversed(range(n_pages)):
        zs.append(_dot(q2, k_refs[p][...].astype(BF16)) + bias)
    sp = jnp.concatenate([_split_bf16(_softplus2(z)) for z in zs], axis=0)
    r = _dot(sp, uu_ref[...])

    carry = jnp.zeros((rows, LANES), F32)
    acc = jnp.zeros((rows, SB_WIDTH), F32)
    for i, z in enumerate(zs):
        ri = r[i * rows:(i + 1) * rows]
        e = jnp.exp2(z - (ri[:, :LANES] + carry)).astype(BF16)
        carry = carry + ri[:, LANES:]
        if i == 0:
            acc = acc + _dot(e, vpad_ref[...].astype(BF16))
        else:
            acc = acc + _dot_nt(e, v_refs[n_pages - i][...].astype(BF16))

    own = jnp.where(own_head, acc, 0.0)
    o_ref[...] = jnp.sum(own.reshape(DEC_SEQ, N_HEADS, SB_WIDTH), axis=1)


def _attn_sample_call(page_table, qb, k_new, v_new, cache_k, cache_v, layer, bias_s, uu):
    n_seq, n_pages = page_table.shape
    rows = DEC_SEQ * N_HEADS
    new = lambda: pl.BlockSpec((None, DEC_SEQ, SB_WIDTH), lambda b, pt: (b, 0, 0))

    def page(p):
        return pl.BlockSpec((None, None, SB_WIDTH, CHUNK), lambda b, pt: (layer, pt[b, p], 0, 0))

    grid_spec = pltpu.PrefetchScalarGridSpec(
        num_scalar_prefetch=1,
        grid=(n_seq,),
        in_specs=[new(), new(), new()] + [page(p) for p in range(n_pages)] * 2 + [
            pl.BlockSpec((rows, LANES), lambda b, pt: (0, 0)),
            pl.BlockSpec((2 * LANES, 2 * LANES), lambda b, pt: (0, 0)),
        ],
        out_specs=pl.BlockSpec((None, DEC_SEQ, SB_WIDTH), lambda b, pt: (b, 0, 0)),
        scratch_shapes=[
            pltpu.VMEM((CHUNK, SB_WIDTH), F32),
            pltpu.VMEM((CHUNK, SB_WIDTH), F32),
        ],
    )
    return pl.pallas_call(
        functools.partial(_attn_sample_kernel, n_pages=n_pages),
        grid_spec=grid_spec,
        out_shape=jax.ShapeDtypeStruct((n_seq, DEC_SEQ, SB_WIDTH), F32),
        compiler_params=pltpu.CompilerParams(
            dimension_semantics=("arbitrary",), vmem_limit_bytes=VMEM_LIMIT),
        name="attn_sample",
    )(page_table, qb, k_new, v_new, *([cache_k] * n_pages), *([cache_v] * n_pages), bias_s, uu)


def _gate_prompt(gvn, ws_ref, gbias):
    tm = gvn.shape[0]
    lane = lax.broadcasted_iota(jnp.int32, (CHUNK, LANES), 1)
    row = lax.broadcasted_iota(jnp.int32, (CHUNK, LANES), 0)
    first_group = lane < HEAD_DIM
    keep = lane <= row
    out_cols = []
    for gp in range(N_PAIRS):
        w_pair = jnp.concatenate(
            [jnp.where(keep, ws_ref[2 * gp], 0.0), jnp.where(keep, ws_ref[2 * gp + 1], 0.0)],
            axis=1).astype(BF16)
        chunks = []
        for c in range(tm // CHUNK):
            r = gvn[c * CHUNK:(c + 1) * CHUNK, gp * LANES:(gp + 1) * LANES]
            r2 = jnp.concatenate(
                [jnp.where(first_group, r, 0.0), jnp.where(first_group, 0.0, r)], axis=0).astype(BF16)
            chunks.append(_dot(w_pair, r2) + gbias[:, gp * LANES:(gp + 1) * LANES])
        out_cols.append(jnp.concatenate(chunks, axis=0))
    return jnp.concatenate(out_cols, axis=1)


def _gate_sample(gvn, pat_ref):
    tm = gvn.shape[0]
    g3 = gvn.reshape(tm // 8, 8, GM_WIDTH)
    out = g3 * pat_ref[0][None] + pat_ref[DEC_SEQ][None]
    for k in range(1, DEC_SEQ):
        out = out + pltpu.roll(g3, k, axis=1) * pat_ref[k][None]
    return out.reshape(tm, GM_WIDTH)


def _post_kernel(x_ref, ma_ref, mb_ref, osb_ref, u_ref, gvn_ref, gate_w_ref, gate_b_ref, gnsb_ref,
                 gngm_ref, wout_ref, gff_ref, w1_ref, w2_ref, gfin_ref, *out_refs, sample, final):
    gate1, shift2 = ma_ref[:, :D_MODEL], ma_ref[:, D_MODEL:]
    scale2, gate2 = mb_ref[:, :D_MODEL], mb_ref[:, D_MODEL:]
    x = x_ref[...]
    o_sb = _rms(osb_ref[...], gnsb_ref[...])
    if sample:
        mixed = _gate_sample(gvn_ref[...], gate_w_ref)
    else:
        mixed = _gate_prompt(gvn_ref[...], gate_w_ref, gate_b_ref[...])
    o_gm = _rms(u_ref[...] * mixed, gngm_ref[...])
    cat = jnp.concatenate([o_sb, o_gm], axis=-1).astype(BF16)
    x = x + gate1 * _dot(cat, wout_ref[...])
    h = (_rms(x, gff_ref[...]) * (1.0 + scale2) + shift2).astype(BF16)
    ff = None
    n_split = 4
    cw = D_FF // n_split
    for c in range(n_split):
        a = jnp.maximum(_dot(h, w1_ref[:, c * cw:(c + 1) * cw]), 0.0)
        part = _dot((a * a).astype(BF16), w2_ref[c * cw:(c + 1) * cw, :])
        ff = part if ff is None else ff + part
    x = x + gate2 * ff
    out_refs[0][...] = x
    if final:
        out_refs[1][...] = _rms(x, gfin_ref[...])


def _post_call(x, mods, mod_lead, layer, o_sb, u, gvn, gate_w, gate_b, gn_sb, gn_gm, w_out, g_ff,
               w_ff1, w_ff2, g_final, tm, sample, final):
    nb, t, _ = x.shape
    mod_rows = 1 if mod_lead is None else tm
    tok = lambda w: pl.BlockSpec((None, tm, w), lambda b, i: (b, i, 0))

    def const(shape):
        return pl.BlockSpec(shape, lambda b, i: (0,) * len(shape), pipeline_mode=pl.Buffered(1))

    n_out = 2 if final else 1
    outs = pl.pallas_call(
        functools.partial(_post_kernel, sample=sample, final=final),
        grid=(nb, t // tm),
        in_specs=[
            tok(D_MODEL),
            _mod_spec(mod_rows, 1, mod_lead), _mod_spec(mod_rows, 2, mod_lead),
            tok(SB_WIDTH), tok(GM_WIDTH), tok(GM_WIDTH),
            const(gate_w.shape), const(gate_b.shape),
            const((1, SB_WIDTH)), const((1, GM_WIDTH)),
            _layer_spec(w_out, layer),
            const((1, D_MODEL)),
            _layer_spec(w_ff1, layer), _layer_spec(w_ff2, layer),
            const((1, D_MODEL)),
        ],
        out_specs=[tok(D_MODEL)] * n_out,
        out_shape=[jax.ShapeDtypeStruct((nb, t, D_MODEL), F32)] * n_out,
        compiler_params=pltpu.CompilerParams(
            dimension_semantics=("arbitrary", "arbitrary"), vmem_limit_bytes=VMEM_LIMIT),
        name="post_sample" if sample else "post_prompt",
    )(x, mods, mods, o_sb, u, gvn, gate_w, gate_b, gn_sb.reshape(1, SB_WIDTH),
      gn_gm.reshape(1, GM_WIDTH), w_out, g_ff.reshape(1, D_MODEL), w_ff1, w_ff2,
      g_final.reshape(1, D_MODEL))
    return outs


def _suffix_sum_matrix():
    j = jnp.arange(LANES)[:, None]
    s = jnp.arange(LANES)[None, :]
    half = jnp.concatenate([(j >= s).astype(BF16), jnp.ones((LANES, LANES), BF16)], axis=1)
    return jnp.concatenate([half, half], axis=0)


def _sample_gate_pattern(w_s, b_s):
    w4 = w_s[:, :, :DEC_SEQ, :DEC_SEQ]
    k = jnp.arange(DEC_SEQ)[:, None, None]
    t = jnp.arange(DEC_SEQ)[None, :, None]
    s = jnp.arange(DEC_SEQ)[None, None, :]
    pick = (s == t - k).astype(F32)
    pat = jnp.sum(w4[:, None] * pick[None, :, None], axis=-1)
    pat = jnp.concatenate([pat, b_s[:, None, :, :DEC_SEQ]], axis=1)
    pat = jnp.swapaxes(pat, 2, 3)
    pat = jnp.concatenate([pat, pat], axis=2)
    return jnp.repeat(pat, HEAD_DIM, axis=-1)


def kernel(x_prompt, x_sample, cache_k, cache_v, page_table, c_prompt, c_sample, w_ada, b_ada, g_mix,
           w_in, sb_bias, gm_vnorm, w_s, b_s, gn_sb, gn_gm, w_out, g_ff, w_ff1, w_ff2, g_final):
    depth = w_in.shape[0]
    n_b, seq, _ = x_prompt.shape
    n_seq, dec_seq, _ = x_sample.shape
    n_pool = cache_k.shape[1]
    n_tok_s = n_seq * dec_seq

    mods = _ada_call(jnp.concatenate([jnp.repeat(c_sample, dec_seq, axis=0), c_prompt], axis=0),
                     w_ada, b_ada)
    uu = _suffix_sum_matrix()
    w_a = jnp.concatenate([w_in[:, :, :SB_WIDTH], w_in[:, :, 3 * SB_WIDTH:]], axis=2).astype(BF16)
    w_kv = w_in[:, :, SB_WIDTH:3 * SB_WIDTH].astype(BF16)
    w_kv_t = jnp.swapaxes(w_kv, 1, 2)
    w_out_b = w_out.astype(BF16)
    w1_b = w_ff1.astype(BF16)
    w2_b = w_ff2.astype(BF16)
    bias2 = sb_bias * LOG2_E
    bias_p_all = jnp.broadcast_to(jnp.repeat(bias2, CHUNK, axis=1).reshape(depth, N_PAIRS, 2 * CHUNK, 1),
                                  (depth, N_PAIRS, 2 * CHUNK, LANES))
    bias_s_all = jnp.broadcast_to(jnp.tile(bias2, (1, dec_seq))[:, :, None],
                                  (depth, dec_seq * N_HEADS, LANES))
    gate_b_all = jnp.repeat(jnp.swapaxes(b_s, 1, 2), HEAD_DIM, axis=2)
    gate_pat_all = _sample_gate_pattern(w_s, b_s)
    cache_k = jnp.transpose(cache_k, (0, 1, 3, 4, 2)).reshape(depth, n_pool, SB_WIDTH, CHUNK)
    cache_v = jnp.transpose(cache_v, (0, 1, 3, 4, 2)).reshape(depth, n_pool, SB_WIDTH, CHUNK)

    xp = x_prompt
    xs = x_sample.reshape(1, n_tok_s, D_MODEL)
    gp, ksm, vsm, gsm = [], [], [], []
    yp = ys = k_pages = v_pages = None
    for l in range(depth):
        final = l == depth - 1
        mods_p = mods[l, n_tok_s:][:, None, :]
        bias_p, bias_s, gate_b_p, gate_pat_s = bias_p_all[l], bias_s_all[l], gate_b_all[l], gate_pat_all[l]

        qb, u, gvn, k_pages, v_pages, kb, vb = _pre_call(
            xp, mods_p, None, l, g_mix[l], w_a, w_kv_t, gm_vnorm[l], 512,
            paged=(depth, k_pages, v_pages))
        o_sb = _attn_prompt_call(qb, kb, vb, bias_p, uu)
        outs = _post_call(xp, mods_p, None, l, o_sb, u, gvn, w_s[l], gate_b_p, gn_sb[l], gn_gm[l],
                          w_out_b, g_ff[l], w1_b, w2_b, g_final, 512, False, final)
        xp = outs[0]
        if final:
            yp = outs[1]
        gp.append(gvn[:, seq - CHUNK:].reshape(n_b, CHUNK, N_HEADS, HEAD_DIM))

        qb, u, gvn, k, v, _, _ = _pre_call(xs, mods, l, l, g_mix[l], w_a, w_kv, gm_vnorm[l], n_tok_s)
        o_sb = _attn_sample_call(page_table, qb.reshape(n_seq, dec_seq, SB_WIDTH),
                                 k.reshape(n_seq, dec_seq, SB_WIDTH), v.reshape(n_seq, dec_seq, SB_WIDTH),
                                 cache_k, cache_v, l, bias_s, uu)
        outs = _post_call(xs, mods, l, l, o_sb.reshape(1, n_tok_s, SB_WIDTH), u, gvn, gate_pat_s,
                          gate_b_p, gn_sb[l], gn_gm[l], w_out_b, g_ff[l], w1_b, w2_b, g_final,
                          n_tok_s, True, final)
        xs = outs[0]
        if final:
            ys = outs[1]
        ksm.append(k.reshape(n_seq, dec_seq, N_HEADS, HEAD_DIM))
        vsm.append(v.reshape(n_seq, dec_seq, N_HEADS, HEAD_DIM))
        gsm.append(gvn.reshape(n_seq, dec_seq, N_HEADS, HEAD_DIM))

    def unpage(pages):
        p = pages.reshape(depth, n_b, seq // CHUNK, N_HEADS, HEAD_DIM, CHUNK)
        return jnp.transpose(p, (0, 1, 2, 5, 3, 4))

    return (yp, ys.reshape(n_seq, dec_seq, D_MODEL), unpage(k_pages), unpage(v_pages), jnp.stack(gp),
            jnp.stack(ksm), jnp.stack(vsm), jnp.stack(gsm))
```

```python
import functools

import jax
import jax.numpy as jnp
from jax import lax
from jax.experimental import pallas as pl
from jax.experimental.pallas import tpu as pltpu

D_MODEL = 1024
SB_WIDTH = 512
GM_WIDTH = 512
HEAD_DIM = 64
N_HEADS = 8
N_PAIRS = 4
IN_WIDTH = 3 * SB_WIDTH + 2 * GM_WIDTH
D_FF = 4 * D_MODEL
CHUNK = 128
DEC_SEQ = 4
EPS = 1e-6
LOG2_E = 1.4426950408889634
MASKED = -1e30
LANES = 128
UNROLLS = (4, 2, 1)
VMEM_LIMIT = 56 * 1024 * 1024

F32 = jnp.float32
BF16 = jnp.bfloat16


def _rms(x, g):
    ms = jnp.mean(x * x, axis=-1, keepdims=True)
    return x * lax.rsqrt(ms + EPS) * g


def _softplus2(z):
    return jnp.maximum(z, 0.0) + jnp.log(1.0 + jnp.exp2(-jnp.abs(z))) * LOG2_E


def _split_bf16(x):
    hi = x.astype(BF16)
    lo = (x - hi.astype(F32)).astype(BF16)
    return jnp.concatenate([hi, lo], axis=1)


def _dot(a, b):
    return jnp.dot(a, b, preferred_element_type=F32)


def _dot_nt(a, b):
    return lax.dot_general(a, b, (((1,), (1,)), ((), ())), preferred_element_type=F32)


def _ada_kernel(c_ref, w_ref, b_ref, o_ref):
    c = c_ref[...]
    s = c * jax.nn.sigmoid(c)
    o_ref[0] = _dot(s.astype(BF16), w_ref[0].astype(BF16)) + b_ref[0]


def _ada_call(c_all, w_ada, b_ada):
    depth, _, n_out = w_ada.shape
    rows = c_all.shape[0]
    tn = 1024
    return pl.pallas_call(
        _ada_kernel,
        grid=(depth, n_out // tn),
        in_specs=[
            pl.BlockSpec((rows, D_MODEL), lambda l, j: (0, 0)),
            pl.BlockSpec((1, D_MODEL, tn), lambda l, j: (l, 0, j)),
            pl.BlockSpec((1, 1, tn), lambda l, j: (l, 0, j)),
        ],
        out_specs=pl.BlockSpec((1, rows, tn), lambda l, j: (l, 0, j)),
        out_shape=jax.ShapeDtypeStruct((depth, rows, n_out), F32),
        compiler_params=pltpu.CompilerParams(
            dimension_semantics=("arbitrary", "arbitrary"), vmem_limit_bytes=VMEM_LIMIT),
        name="ada",
    )(c_all, w_ada, b_ada.reshape(depth, 1, n_out))


def _pre_kernel(x_ref, m_ref, gmix_ref, wa_ref, wkv_ref, gvnorm_ref, *refs, pages):
    q_ref, u_ref, gvn_ref, k_ref, v_ref, kb_ref, vb_ref = refs[-7:]
    x = x_ref[...]
    shift, scale = m_ref[:, :D_MODEL], m_ref[:, D_MODEL:]
    h = (_rms(x, gmix_ref[...]) * (1.0 + scale) + shift).astype(BF16)
    pa = _dot(h, wa_ref[...])
    q_ref[...] = (pa[:, :SB_WIDTH] * (HEAD_DIM ** -0.5 * LOG2_E)).astype(BF16)
    u_ref[...] = pa[:, SB_WIDTH:SB_WIDTH + GM_WIDTH]
    gvn_ref[...] = _rms(pa[:, SB_WIDTH + GM_WIDTH:], gvnorm_ref[...])
    if pages:
        kv = _dot_nt(wkv_ref[...], h)
        for c in range(pages):
            blk = kv[:, c * CHUNK:(c + 1) * CHUNK]
            k_ref[c] = blk[:SB_WIDTH]
            v_ref[c] = blk[SB_WIDTH:]
            kb_ref[c] = blk[:SB_WIDTH].astype(BF16)
            vb_ref[c] = blk[SB_WIDTH:].astype(BF16)
    else:
        kv = _dot(h, wkv_ref[...])
        k_ref[...] = kv[:, :SB_WIDTH]
        v_ref[...] = kv[:, SB_WIDTH:]
        kb_ref[...] = kv[:, :SB_WIDTH].astype(BF16)
        vb_ref[...] = kv[:, SB_WIDTH:].astype(BF16)


def _mod_spec(rows, pair, lead):
    if lead is None:
        return pl.BlockSpec((None, rows, 2 * D_MODEL), lambda b, i: (b, 0, pair))
    return pl.BlockSpec((None, rows, 2 * D_MODEL), lambda b, i: (lead, 0, pair),
                        pipeline_mode=pl.Buffered(1))


def _layer_spec(w, layer):
    return pl.BlockSpec((None,) + w.shape[1:], lambda b, i: (layer,) + (0,) * (w.ndim - 1),
                        pipeline_mode=pl.Buffered(1))


def _pre_call(x, mods, mod_lead, layer, g_mix, w_a, w_kv, gm_vnorm, tm, paged=None):
    nb, t, _ = x.shape
    tok = lambda w: pl.BlockSpec((None, tm, w), lambda b, i: (b, i, 0))
    const = lambda shape: pl.BlockSpec(shape, lambda b, i: (0,) * len(shape))
    out_specs = [tok(SB_WIDTH), tok(GM_WIDTH), tok(GM_WIDTH)]
    out_shape = [jax.ShapeDtypeStruct((nb, t, SB_WIDTH), BF16),
                 jax.ShapeDtypeStruct((nb, t, GM_WIDTH), F32),
                 jax.ShapeDtypeStruct((nb, t, GM_WIDTH), F32)]
    operands = [x, mods, g_mix.reshape(1, D_MODEL), w_a, w_kv, gm_vnorm.reshape(1, GM_WIDTH)]
    in_specs = [tok(D_MODEL), _mod_spec(1 if mod_lead is None else tm, 0, mod_lead),
                const((1, D_MODEL)), _layer_spec(w_a, layer), _layer_spec(w_kv, layer),
                const((1, GM_WIDTH))]
    aliases = {}
    if paged is None:
        pages = 0
        out_specs += [tok(SB_WIDTH)] * 4
        out_shape += [jax.ShapeDtypeStruct((nb, t, SB_WIDTH), dt) for dt in (F32, F32, BF16, BF16)]
    else:
        depth, k_all, v_all = paged
        pages = tm // CHUNK
        n_pages = t // CHUNK
        page_shape = (pages, SB_WIDTH, CHUNK)
        out_specs += [pl.BlockSpec((None, None) + page_shape, lambda b, i: (layer, b, i, 0, 0))] * 2
        out_specs += [pl.BlockSpec((None,) + page_shape, lambda b, i: (b, i, 0, 0))] * 2
        out_shape += [jax.ShapeDtypeStruct((depth, nb, n_pages, SB_WIDTH, CHUNK), F32)] * 2
        out_shape += [jax.ShapeDtypeStruct((nb, n_pages, SB_WIDTH, CHUNK), BF16)] * 2
        if k_all is not None:
            aliases = {len(operands): 3, len(operands) + 1: 4}
            operands += [k_all, v_all]
            in_specs += [pl.BlockSpec(memory_space=pl.ANY)] * 2
    return pl.pallas_call(
        functools.partial(_pre_kernel, pages=pages),
        grid=(nb, t // tm),
        in_specs=in_specs,
        out_specs=out_specs,
        out_shape=out_shape,
        input_output_aliases=aliases,
        compiler_params=pltpu.CompilerParams(
            dimension_semantics=("arbitrary", "arbitrary"), vmem_limit_bytes=VMEM_LIMIT),
        name="pre",
    )(*operands)


def _prompt_attention(qi, q_ref, k_ref, v_ref, bias_ref, uu_ref, o_ref,
                      q2_ref, z_ref, sp_ref, acc_ref, carry_ref):
    lane = lax.broadcasted_iota(jnp.int32, (CHUNK, LANES), 1)
    row = lax.broadcasted_iota(jnp.int32, (CHUNK, LANES), 0)
    first_head = lane < HEAD_DIM
    causal = lane < row
    causal2 = jnp.concatenate([causal, causal], axis=0)
    cols = [slice(hp * LANES, (hp + 1) * LANES) for hp in range(N_PAIRS)]

    qf = q_ref[...].astype(F32)
    for hp in range(N_PAIRS):
        qp = qf[:, cols[hp]]
        q2_ref[hp] = jnp.concatenate(
            [jnp.where(first_head, qp, 0.0), jnp.where(first_head, 0.0, qp)], axis=0).astype(BF16)
        acc_ref[hp] = jnp.zeros((2 * CHUNK, LANES), F32)
        carry_ref[hp] = jnp.zeros((2 * CHUNK, LANES), F32)

    def scores(kb):
        ks = k_ref[kb]
        return [_dot(q2_ref[hp], ks[cols[hp], :]) + bias_ref[hp] for hp in range(N_PAIRS)]

    def stash(zs):
        for hp in range(N_PAIRS):
            z_ref[hp] = zs[hp]
            sp_ref[hp] = _softplus2(zs[hp]).astype(BF16)

    def finish(kb, score_next):
        vs = v_ref[kb]
        zs = [z_ref[hp] for hp in range(N_PAIRS)]
        rs = [_dot(sp_ref[hp], uu_ref[:LANES, :]) for hp in range(N_PAIRS)]
        zs_next = scores(kb - 1) if score_next else None
        es = []
        for hp in range(N_PAIRS):
            carry = carry_ref[hp]
            es.append(jnp.exp2(zs[hp] - (rs[hp][:, :LANES] + carry)).astype(BF16))
            carry_ref[hp] = carry + rs[hp][:, LANES:]
        for hp in range(N_PAIRS):
            acc_ref[hp] += _dot_nt(es[hp], vs[cols[hp], :])
        if score_next:
            stash(zs_next)

    stash([jnp.where(causal2, z, MASKED) for z in scores(qi)])

    top = qi
    for width in UNROLLS:
        def body(i, c, top=top, width=width):
            for j in range(width):
                finish(top - width * i - j, True)
            return c

        trips = top // width
        lax.fori_loop(0, trips, body, 0)
        top = top - trips * width
    finish(0, False)

    for hp in range(N_PAIRS):
        a = acc_ref[hp]
        o_ref[:, cols[hp]] = jnp.where(first_head, a[:CHUNK], a[CHUNK:])


def _sample_attention(q_ref, kn_ref, vn_ref, k_refs, v_refs, bias_ref, uu_ref, o_ref,
                      kpad_ref, vpad_ref):
    n_pages = len(k_refs)
    rows = DEC_SEQ * N_HEADS

    lane = lax.broadcasted_iota(jnp.int32, (rows, SB_WIDTH), 1)
    row = lax.broadcasted_iota(jnp.int32, (rows, SB_WIDTH), 0)
    own_head = lane // HEAD_DIM == row % N_HEADS
    qf = q_ref[...].astype(F32)
    qrep = jnp.broadcast_to(qf[:, None, :], (DEC_SEQ, N_HEADS, SB_WIDTH)).reshape(rows, SB_WIDTH)
    q2 = jnp.where(own_head, qrep, 0.0).astype(BF16)

    kpad_ref[...] = jnp.zeros_like(kpad_ref)
    vpad_ref[...] = jnp.zeros_like(vpad_ref)
    kpad_ref[0:DEC_SEQ, :] = kn_ref[...]
    vpad_ref[0:DEC_SEQ, :] = vn_ref[...]
    col = lax.broadcasted_iota(jnp.int32, (rows, LANES), 1)
    qpos = lax.broadcasted_iota(jnp.int32, (rows, LANES), 0) // N_HEADS
    bias = bias_ref[...]

    zs = [jnp.where(col < qpos, _dot_nt(q2, kpad_ref[...].astype(BF16)) + bias, MASKED)]
    for p in reversed(range(n_pages)):
        zs.append(_dot(q2, k_refs[p][...].astype(BF16)) + bias)
    sp = jnp.concatenate([_split_bf16(_softplus2(z)) for z in zs], axis=0)
    r = _dot(sp, uu_ref[...])

    carry = jnp.zeros((rows, LANES), F32)
    acc = jnp.zeros((rows, SB_WIDTH), F32)
    for i, z in enumerate(zs):
        ri = r[i * rows:(i + 1) * rows]
        e = jnp.exp2(z - (ri[:, :LANES] + carry)).astype(BF16)
        carry = carry + ri[:, LANES:]
        if i == 0:
            acc = acc + _dot(e, vpad_ref[...].astype(BF16))
        else:
            acc = acc + _dot_nt(e, v_refs[n_pages - i][...].astype(BF16))

    own = jnp.where(own_head, acc, 0.0)
    o_ref[...] = jnp.sum(own.reshape(DEC_SEQ, N_HEADS, SB_WIDTH), axis=1)


def _attn_kernel(pt_ref, qp_ref, kp_ref, vp_ref, biasp_ref, uu_ref, qs_ref, kn_ref, vn_ref, *refs,
                 n_pages):
    k_refs, v_refs = refs[:n_pages], refs[n_pages:2 * n_pages]
    (biass_ref, op_ref, os_ref,
     q2_ref, z_ref, sp_ref, acc_ref, carry_ref, kpad_ref, vpad_ref) = refs[2 * n_pages:]
    _sample_attention(qs_ref, kn_ref, vn_ref, k_refs, v_refs, biass_ref, uu_ref, os_ref,
                      kpad_ref, vpad_ref)
    _prompt_attention(pl.program_id(1), qp_ref, kp_ref, vp_ref, biasp_ref, uu_ref, op_ref,
                      q2_ref, z_ref, sp_ref, acc_ref, carry_ref)


def _attn_call(qb, kb, vb, bias_p, page_table, qs, k_new, v_new, cache_k, cache_v, layer, bias_s, uu):
    nb, t, _ = qb.shape
    blocks = t // CHUNK
    n_seq, n_pages = page_table.shape
    assert n_seq == nb * blocks, "one sample sequence rides along with each prompt query block"
    rows = DEC_SEQ * N_HEADS
    stacked = (N_PAIRS, 2 * CHUNK, LANES)
    new = lambda: pl.BlockSpec((None, DEC_SEQ, SB_WIDTH), lambda b, i, pt: (b * blocks + i, 0, 0))

    def page(p):
        return pl.BlockSpec((None, None, SB_WIDTH, CHUNK),
                            lambda b, i, pt: (layer, pt[b * blocks + i, p], 0, 0))

    grid_spec = pltpu.PrefetchScalarGridSpec(
        num_scalar_prefetch=1,
        grid=(nb, blocks),
        in_specs=[
            pl.BlockSpec((None, CHUNK, SB_WIDTH), lambda b, i, pt: (b, i, 0)),
            pl.BlockSpec((None,) + kb.shape[1:], lambda b, i, pt: (b, 0, 0, 0)),
            pl.BlockSpec((None,) + vb.shape[1:], lambda b, i, pt: (b, 0, 0, 0)),
            pl.BlockSpec(stacked, lambda b, i, pt: (0, 0, 0)),
            pl.BlockSpec((2 * LANES, 2 * LANES), lambda b, i, pt: (0, 0)),
            new(), new(), new(),
        ] + [page(p) for p in range(n_pages)] * 2 + [
            pl.BlockSpec((rows, LANES), lambda b, i, pt: (0, 0)),
        ],
        out_specs=[
            pl.BlockSpec((None, CHUNK, SB_WIDTH), lambda b, i, pt: (b, i, 0)),
            pl.BlockSpec((None, DEC_SEQ, SB_WIDTH), lambda b, i, pt: (b * blocks + i, 0, 0)),
        ],
        scratch_shapes=[
            pltpu.VMEM(stacked, BF16),
            pltpu.VMEM(stacked, F32),
            pltpu.VMEM(stacked, BF16),
            pltpu.VMEM(stacked, F32),
            pltpu.VMEM(stacked, F32),
            pltpu.VMEM((CHUNK, SB_WIDTH), F32),
            pltpu.VMEM((CHUNK, SB_WIDTH), F32),
        ],
    )
    return pl.pallas_call(
        functools.partial(_attn_kernel, n_pages=n_pages),
        grid_spec=grid_spec,
        out_shape=[jax.ShapeDtypeStruct((nb, t, SB_WIDTH), F32),
                   jax.ShapeDtypeStruct((n_seq, DEC_SEQ, SB_WIDTH), F32)],
        compiler_params=pltpu.CompilerParams(
            dimension_semantics=("arbitrary", "arbitrary"), vmem_limit_bytes=VMEM_LIMIT),
        name="attn",
    )(page_table, qb, kb, vb, bias_p, uu, qs, k_new, v_new,
      *([cache_k] * n_pages), *([cache_v] * n_pages), bias_s)


def _gate_prompt(gvn, ws_ref, gbias):
    tm = gvn.shape[0]
    lane = lax.broadcasted_iota(jnp.int32, (CHUNK, LANES), 1)
    row = lax.broadcasted_iota(jnp.int32, (CHUNK, LANES), 0)
    first_group = lane < HEAD_DIM
    keep = lane <= row
    out_cols = []
    for gp in range(N_PAIRS):
        w_pair = jnp.concatenate(
            [jnp.where(keep, ws_ref[2 * gp], 0.0), jnp.where(keep, ws_ref[2 * gp + 1], 0.0)],
            axis=1).astype(BF16)
        chunks = []
        for c in range(tm // CHUNK):
            r = gvn[c * CHUNK:(c + 1) * CHUNK, gp * LANES:(gp + 1) * LANES]
            r2 = jnp.concatenate(
                [jnp.where(first_group, r, 0.0), jnp.where(first_group, 0.0, r)], axis=0).astype(BF16)
            chunks.append(_dot(w_pair, r2) + gbias[:, gp * LANES:(gp + 1) * LANES])
        out_cols.append(jnp.concatenate(chunks, axis=0))
    return jnp.concatenate(out_cols, axis=1)


def _gate_sample(gvn, pat_ref):
    tm = gvn.shape[0]
    g3 = gvn.reshape(tm // 8, 8, GM_WIDTH)
    out = g3 * pat_ref[0][None] + pat_ref[DEC_SEQ][None]
    for k in range(1, DEC_SEQ):
        out = out + pltpu.roll(g3, k, axis=1) * pat_ref[k][None]
    return out.reshape(tm, GM_WIDTH)


def _post_kernel(x_ref, ma_ref, mb_ref, osb_ref, u_ref, gvn_ref, gate_w_ref, gate_b_ref, gnsb_ref,
                 gngm_ref, wout_ref, gff_ref, w1_ref, w2_ref, gfin_ref, *out_refs, sample, final):
    gate1, shift2 = ma_ref[:, :D_MODEL], ma_ref[:, D_MODEL:]
    scale2, gate2 = mb_ref[:, :D_MODEL], mb_ref[:, D_MODEL:]
    x = x_ref[...]
    o_sb = _rms(osb_ref[...], gnsb_ref[...])
    if sample:
        mixed = _gate_sample(gvn_ref[...], gate_w_ref)
    else:
        mixed = _gate_prompt(gvn_ref[...], gate_w_ref, gate_b_ref[...])
    o_gm = _rms(u_ref[...] * mixed, gngm_ref[...])
    cat = jnp.concatenate([o_sb, o_gm], axis=-1).astype(BF16)
    x = x + gate1 * _dot(cat, wout_ref[...])
    h = (_rms(x, gff_ref[...]) * (1.0 + scale2) + shift2).astype(BF16)
    ff = None
    n_split = 4
    cw = D_FF // n_split
    for c in range(n_split):
        a = jnp.maximum(_dot(h, w1_ref[:, c * cw:(c + 1) * cw]), 0.0)
        part = _dot((a * a).astype(BF16), w2_ref[c * cw:(c + 1) * cw, :])
        ff = part if ff is None else ff + part
    x = x + gate2 * ff
    out_refs[0][...] = x
    if final:
        out_refs[1][...] = _rms(x, gfin_ref[...])


def _post_call(x, mods, mod_lead, layer, o_sb, u, gvn, gate_w, gate_b, gn_sb, gn_gm, w_out, g_ff,
               w_ff1, w_ff2, g_final, tm, sample, final):
    nb, t, _ = x.shape
    mod_rows = 1 if mod_lead is None else tm
    tok = lambda w: pl.BlockSpec((None, tm, w), lambda b, i: (b, i, 0))

    def const(shape):
        return pl.BlockSpec(shape, lambda b, i: (0,) * len(shape), pipeline_mode=pl.Buffered(1))

    n_out = 2 if final else 1
    outs = pl.pallas_call(
        functools.partial(_post_kernel, sample=sample, final=final),
        grid=(nb, t // tm),
        in_specs=[
            tok(D_MODEL),
            _mod_spec(mod_rows, 1, mod_lead), _mod_spec(mod_rows, 2, mod_lead),
            tok(SB_WIDTH), tok(GM_WIDTH), tok(GM_WIDTH),
            const(gate_w.shape), const(gate_b.shape),
            const((1, SB_WIDTH)), const((1, GM_WIDTH)),
            _layer_spec(w_out, layer),
            const((1, D_MODEL)),
            _layer_spec(w_ff1, layer), _layer_spec(w_ff2, layer),
            const((1, D_MODEL)),
        ],
        out_specs=[tok(D_MODEL)] * n_out,
        out_shape=[jax.ShapeDtypeStruct((nb, t, D_MODEL), F32)] * n_out,
        compiler_params=pltpu.CompilerParams(
            dimension_semantics=("arbitrary", "arbitrary"), vmem_limit_bytes=VMEM_LIMIT),
        name="post_sample" if sample else "post_prompt",
    )(x, mods, mods, o_sb, u, gvn, gate_w, gate_b, gn_sb.reshape(1, SB_WIDTH),
      gn_gm.reshape(1, GM_WIDTH), w_out, g_ff.reshape(1, D_MODEL), w_ff1, w_ff2,
      g_final.reshape(1, D_MODEL))
    return outs


def _suffix_sum_matrix():
    j = jnp.arange(LANES)[:, None]
    s = jnp.arange(LANES)[None, :]
    half = jnp.concatenate([(j >= s).astype(BF16), jnp.ones((LANES, LANES), BF16)], axis=1)
    return jnp.concatenate([half, half], axis=0)


def _sample_gate_pattern(w_s, b_s):
    w4 = w_s[:, :, :DEC_SEQ, :DEC_SEQ]
    k = jnp.arange(DEC_SEQ)[:, None, None]
    t = jnp.arange(DEC_SEQ)[None, :, None]
    s = jnp.arange(DEC_SEQ)[None, None, :]
    pick = (s == t - k).astype(F32)
    pat = jnp.sum(w4[:, None] * pick[None, :, None], axis=-1)
    pat = jnp.concatenate([pat, b_s[:, None, :, :DEC_SEQ]], axis=1)
    pat = jnp.swapaxes(pat, 2, 3)
    pat = jnp.concatenate([pat, pat], axis=2)
    return jnp.repeat(pat, HEAD_DIM, axis=-1)


def kernel(x_prompt, x_sample, cache_k, cache_v, page_table, c_prompt, c_sample, w_ada, b_ada, g_mix,
           w_in, sb_bias, gm_vnorm, w_s, b_s, gn_sb, gn_gm, w_out, g_ff, w_ff1, w_ff2, g_final):
    depth = w_in.shape[0]
    n_b, seq, _ = x_prompt.shape
    n_seq, dec_seq, _ = x_sample.shape
    n_pool = cache_k.shape[1]
    n_tok_s = n_seq * dec_seq

    mods = _ada_call(jnp.concatenate([jnp.repeat(c_sample, dec_seq, axis=0), c_prompt], axis=0),
                     w_ada, b_ada)
    uu = _suffix_sum_matrix()
    w_a = jnp.concatenate([w_in[:, :, :SB_WIDTH], w_in[:, :, 3 * SB_WIDTH:]], axis=2).astype(BF16)
    w_kv = w_in[:, :, SB_WIDTH:3 * SB_WIDTH].astype(BF16)
    w_kv_t = jnp.swapaxes(w_kv, 1, 2)
    w_out_b = w_out.astype(BF16)
    w1_b = w_ff1.astype(BF16)
    w2_b = w_ff2.astype(BF16)
    bias2 = sb_bias * LOG2_E
    bias_p_all = jnp.broadcast_to(jnp.repeat(bias2, CHUNK, axis=1).reshape(depth, N_PAIRS, 2 * CHUNK, 1),
                                  (depth, N_PAIRS, 2 * CHUNK, LANES))
    bias_s_all = jnp.broadcast_to(jnp.tile(bias2, (1, dec_seq))[:, :, None],
                                  (depth, dec_seq * N_HEADS, LANES))
    gate_b_all = jnp.repeat(jnp.swapaxes(b_s, 1, 2), HEAD_DIM, axis=2)
    gate_pat_all = _sample_gate_pattern(w_s, b_s)
    cache_k = jnp.transpose(cache_k, (0, 1, 3, 4, 2)).reshape(depth, n_pool, SB_WIDTH, CHUNK)
    cache_v = jnp.transpose(cache_v, (0, 1, 3, 4, 2)).reshape(depth, n_pool, SB_WIDTH, CHUNK)

    xp = x_prompt
    xs = x_sample.reshape(1, n_tok_s, D_MODEL)
    gp, ksm, vsm, gsm = [], [], [], []
    yp = ys = k_pages = v_pages = None
    for l in range(depth):
        final = l == depth - 1
        mods_p = mods[l, n_tok_s:][:, None, :]
        bias_p, bias_s, gate_b_p, gate_pat_s = bias_p_all[l], bias_s_all[l], gate_b_all[l], gate_pat_all[l]

        qb, u, gvn, k_pages, v_pages, kb, vb = _pre_call(
            xp, mods_p, None, l, g_mix[l], w_a, w_kv_t, gm_vnorm[l], 512,
            paged=(depth, k_pages, v_pages))
        qb_s, u_s, gvn_s, k_s, v_s, _, _ = _pre_call(
            xs, mods, l, l, g_mix[l], w_a, w_kv, gm_vnorm[l], n_tok_s)
        per_seq = lambda a: a.reshape(n_seq, dec_seq, SB_WIDTH)
        o_sb, o_sb_s = _attn_call(qb, kb, vb, bias_p, page_table, per_seq(qb_s), per_seq(k_s),
                                  per_seq(v_s), cache_k, cache_v, l, bias_s, uu)

        outs = _post_call(xp, mods_p, None, l, o_sb, u, gvn, w_s[l], gate_b_p, gn_sb[l], gn_gm[l],
                          w_out_b, g_ff[l], w1_b, w2_b, g_final, 512, False, final)
        xp = outs[0]
        if final:
            yp = outs[1]
        gp.append(gvn[:, seq - CHUNK:].reshape(n_b, CHUNK, N_HEADS, HEAD_DIM))

        outs = _post_call(xs, mods, l, l, o_sb_s.reshape(1, n_tok_s, SB_WIDTH), u_s, gvn_s, gate_pat_s,
                          gate_b_p, gn_sb[l], gn_gm[l], w_out_b, g_ff[l], w1_b, w2_b, g_final,
                          n_tok_s, True, final)
        xs = outs[0]
        if final:
            ys = outs[1]
        ksm.append(k_s.reshape(n_seq, dec_seq, N_HEADS, HEAD_DIM))
        vsm.append(v_s.reshape(n_seq, dec_seq, N_HEADS, HEAD_DIM))
        gsm.append(gvn_s.reshape(n_seq, dec_seq, N_HEADS, HEAD_DIM))

    def unpage(pages):
        p = pages.reshape(depth, n_b, seq // CHUNK, N_HEADS, HEAD_DIM, CHUNK)
        return jnp.transpose(p, (0, 1, 2, 5, 3, 4))

    return (yp, ys.reshape(n_seq, dec_seq, D_MODEL), unpage(k_pages), unpage(v_pages), jnp.stack(gp),
            jnp.stack(ksm), jnp.stack(vsm), jnp.stack(gsm))
```

```python
import functools

import jax
import jax.numpy as jnp
from jax import lax
from jax.experimental import pallas as pl
from jax.experimental.pallas import tpu as pltpu

D_MODEL = 1024
SB_WIDTH = 512
GM_WIDTH = 512
HEAD_DIM = 64
N_HEADS = 8
N_PAIRS = 4
IN_WIDTH = 3 * SB_WIDTH + 2 * GM_WIDTH
D_FF = 4 * D_MODEL
CHUNK = 128
DEC_SEQ = 4
EPS = 1e-6
LOG2_E = 1.4426950408889634
MASKED = -1e30
LANES = 128
UNROLLS = (4, 2, 1)
VMEM_LIMIT = 56 * 1024 * 1024

F32 = jnp.float32
BF16 = jnp.bfloat16


def _rms(x, g):
    ms = jnp.mean(x * x, axis=-1, keepdims=True)
    return x * lax.rsqrt(ms + EPS) * g


def _softplus2(z):
    return jnp.maximum(z, 0.0) + jnp.log(1.0 + jnp.exp2(-jnp.abs(z))) * LOG2_E


def _split_bf16(x):
    hi = x.astype(BF16)
    lo = (x - hi.astype(F32)).astype(BF16)
    return jnp.concatenate([hi, lo], axis=1)


def _dot(a, b):
    return jnp.dot(a, b, preferred_element_type=F32)


def _dot_nt(a, b):
    return lax.dot_general(a, b, (((1,), (1,)), ((), ())), preferred_element_type=F32)


def _ada_kernel(c_ref, w_ref, b_ref, o_ref):
    c = c_ref[...]
    s = c * jax.nn.sigmoid(c)
    o_ref[0] = _dot(s.astype(BF16), w_ref[0].astype(BF16)) + b_ref[0]


def _ada_call(c_all, w_ada, b_ada):
    depth, _, n_out = w_ada.shape
    rows = c_all.shape[0]
    tn = 1024
    return pl.pallas_call(
        _ada_kernel,
        grid=(depth, n_out // tn),
        in_specs=[
            pl.BlockSpec((rows, D_MODEL), lambda l, j: (0, 0)),
            pl.BlockSpec((1, D_MODEL, tn), lambda l, j: (l, 0, j)),
            pl.BlockSpec((1, 1, tn), lambda l, j: (l, 0, j)),
        ],
        out_specs=pl.BlockSpec((1, rows, tn), lambda l, j: (l, 0, j)),
        out_shape=jax.ShapeDtypeStruct((depth, rows, n_out), F32),
        compiler_params=pltpu.CompilerParams(
            dimension_semantics=("arbitrary", "arbitrary"), vmem_limit_bytes=VMEM_LIMIT),
        name="ada",
    )(c_all, w_ada, b_ada.reshape(depth, 1, n_out))


def _pre_kernel(x_ref, m_ref, gmix_ref, wa_ref, wkv_ref, gvnorm_ref, *refs, pages):
    q_ref, u_ref, gvn_ref, k_ref, v_ref, kb_ref, vb_ref = refs[-7:]
    x = x_ref[...]
    shift, scale = m_ref[:, :D_MODEL], m_ref[:, D_MODEL:]
    h = (_rms(x, gmix_ref[...]) * (1.0 + scale) + shift).astype(BF16)
    pa = _dot(h, wa_ref[...])
    q_ref[...] = (pa[:, :SB_WIDTH] * (HEAD_DIM ** -0.5 * LOG2_E)).astype(BF16)
    u_ref[...] = pa[:, SB_WIDTH:SB_WIDTH + GM_WIDTH]
    gvn_ref[...] = _rms(pa[:, SB_WIDTH + GM_WIDTH:], gvnorm_ref[...])
    if pages:
        kv = _dot_nt(wkv_ref[...], h)
        for c in range(pages):
            blk = kv[:, c * CHUNK:(c + 1) * CHUNK]
            k_ref[c] = blk[:SB_WIDTH]
            v_ref[c] = blk[SB_WIDTH:]
            kb_ref[c] = blk[:SB_WIDTH].astype(BF16)
            vb_ref[c] = blk[SB_WIDTH:].astype(BF16)
    else:
        kv = _dot(h, wkv_ref[...])
        k_ref[...] = kv[:, :SB_WIDTH]
        v_ref[...] = kv[:, SB_WIDTH:]
        kb_ref[...] = kv[:, :SB_WIDTH].astype(BF16)
        vb_ref[...] = kv[:, SB_WIDTH:].astype(BF16)


def _mod_spec(rows, pair, lead):
    if lead is None:
        return pl.BlockSpec((None, rows, 2 * D_MODEL), lambda b, i: (b, 0, pair))
    return pl.BlockSpec((None, rows, 2 * D_MODEL), lambda b, i: (lead, 0, pair),
                        pipeline_mode=pl.Buffered(1))


def _layer_spec(w, layer):
    return pl.BlockSpec((None,) + w.shape[1:], lambda b, i: (layer,) + (0,) * (w.ndim - 1),
                        pipeline_mode=pl.Buffered(1))


def _pre_call(x, mods, mod_lead, layer, g_mix, w_a, w_kv, gm_vnorm, tm, paged=None):
    nb, t, _ = x.shape
    tok = lambda w: pl.BlockSpec((None, tm, w), lambda b, i: (b, i, 0))
    const = lambda shape: pl.BlockSpec(shape, lambda b, i: (0,) * len(shape))
    out_specs = [tok(SB_WIDTH), tok(GM_WIDTH), tok(GM_WIDTH)]
    out_shape = [jax.ShapeDtypeStruct((nb, t, SB_WIDTH), BF16),
                 jax.ShapeDtypeStruct((nb, t, GM_WIDTH), F32),
                 jax.ShapeDtypeStruct((nb, t, GM_WIDTH), F32)]
    operands = [x, mods, g_mix.reshape(1, D_MODEL), w_a, w_kv, gm_vnorm.reshape(1, GM_WIDTH)]
    in_specs = [tok(D_MODEL), _mod_spec(1 if mod_lead is None else tm, 0, mod_lead),
                const((1, D_MODEL)), _layer_spec(w_a, layer), _layer_spec(w_kv, layer),
                const((1, GM_WIDTH))]
    aliases = {}
    if paged is None:
        pages = 0
        out_specs += [tok(SB_WIDTH)] * 4
        out_shape += [jax.ShapeDtypeStruct((nb, t, SB_WIDTH), dt) for dt in (F32, F32, BF16, BF16)]
    else:
        depth, k_all, v_all = paged
        pages = tm // CHUNK
        n_pages = t // CHUNK
        page_shape = (pages, SB_WIDTH, CHUNK)
        out_specs += [pl.BlockSpec((None, None) + page_shape, lambda b, i: (layer, b, i, 0, 0))] * 2
        out_specs += [pl.BlockSpec((None,) + page_shape, lambda b, i: (b, i, 0, 0))] * 2
        out_shape += [jax.ShapeDtypeStruct((depth, nb, n_pages, SB_WIDTH, CHUNK), F32)] * 2
        out_shape += [jax.ShapeDtypeStruct((nb, n_pages, SB_WIDTH, CHUNK), BF16)] * 2
        if k_all is not None:
            aliases = {len(operands): 3, len(operands) + 1: 4}
            operands += [k_all, v_all]
            in_specs += [pl.BlockSpec(memory_space=pl.ANY)] * 2
    return pl.pallas_call(
        functools.partial(_pre_kernel, pages=pages),
        grid=(nb, t // tm),
        in_specs=in_specs,
        out_specs=out_specs,
        out_shape=out_shape,
        input_output_aliases=aliases,
        compiler_params=pltpu.CompilerParams(
            dimension_semantics=("arbitrary", "arbitrary"), vmem_limit_bytes=VMEM_LIMIT),
        name="pre",
    )(*operands)


def _prompt_attention(qi, q_ref, k_ref, v_ref, bias_ref, uu_ref, o_ref,
                      q2_ref, z_ref, sp_ref, acc_ref, carry_ref):
    lane = lax.broadcasted_iota(jnp.int32, (CHUNK, LANES), 1)
    row = lax.broadcasted_iota(jnp.int32, (CHUNK, LANES), 0)
    first_head = lane < HEAD_DIM
    causal = lane < row
    causal2 = jnp.concatenate([causal, causal], axis=0)
    cols = [slice(hp * LANES, (hp + 1) * LANES) for hp in range(N_PAIRS)]

    qf = q_ref[...].astype(F32)
    for hp in range(N_PAIRS):
        qp = qf[:, cols[hp]]
        q2_ref[hp] = jnp.concatenate(
            [jnp.where(first_head, qp, 0.0), jnp.where(first_head, 0.0, qp)], axis=0).astype(BF16)
        acc_ref[hp] = jnp.zeros((2 * CHUNK, LANES), F32)
        carry_ref[hp] = jnp.zeros((2 * CHUNK, LANES), F32)

    def scores(kb):
        ks = k_ref[kb]
        return [_dot(q2_ref[hp], ks[cols[hp], :]) + bias_ref[hp] for hp in range(N_PAIRS)]

    def stash(zs):
        for hp in range(N_PAIRS):
            z_ref[hp] = zs[hp]
            sp_ref[hp] = _softplus2(zs[hp]).astype(BF16)

    def finish(kb):
        vs = v_ref[kb]
        zs = [z_ref[hp] for hp in range(N_PAIRS)]
        rs = [_dot(sp_ref[hp], uu_ref[:LANES, :]) for hp in range(N_PAIRS)]
        zs_next = scores(jnp.maximum(kb - 1, 0))
        es = []
        for hp in range(N_PAIRS):
            carry = carry_ref[hp]
            es.append(jnp.exp2(zs[hp] - (rs[hp][:, :LANES] + carry)).astype(BF16))
            carry_ref[hp] = carry + rs[hp][:, LANES:]
        for hp in range(N_PAIRS):
            acc_ref[hp] += _dot_nt(es[hp], vs[cols[hp], :])
        stash(zs_next)

    stash([jnp.where(causal2, z, MASKED) for z in scores(qi)])

    top = qi
    for width in UNROLLS:
        def body(i, c, top=top, width=width):
            for j in range(width):
                finish(top - width * i - j)
            return c

        trips = (top + 1) // width
        lax.fori_loop(0, trips, body, 0)
        top = top - trips * width

    for hp in range(N_PAIRS):
        a = acc_ref[hp]
        o_ref[:, cols[hp]] = jnp.where(first_head, a[:CHUNK], a[CHUNK:])


def _sample_attention(q_ref, kn_ref, vn_ref, k_refs, v_refs, bias_ref, uu_ref, o_ref,
                      kpad_ref, vpad_ref):
    n_pages = len(k_refs)
    rows = DEC_SEQ * N_HEADS

    lane = lax.broadcasted_iota(jnp.int32, (rows, SB_WIDTH), 1)
    row = lax.broadcasted_iota(jnp.int32, (rows, SB_WIDTH), 0)
    own_head = lane // HEAD_DIM == row % N_HEADS
    qf = q_ref[...].astype(F32)
    qrep = jnp.broadcast_to(qf[:, None, :], (DEC_SEQ, N_HEADS, SB_WIDTH)).reshape(rows, SB_WIDTH)
    q2 = jnp.where(own_head, qrep, 0.0).astype(BF16)

    kpad_ref[...] = jnp.zeros_like(kpad_ref)
    vpad_ref[...] = jnp.zeros_like(vpad_ref)
    kpad_ref[0:DEC_SEQ, :] = kn_ref[...]
    vpad_ref[0:DEC_SEQ, :] = vn_ref[...]
    col = lax.broadcasted_iota(jnp.int32, (rows, LANES), 1)
    qpos = lax.broadcasted_iota(jnp.int32, (rows, LANES), 0) // N_HEADS
    bias = bias_ref[...]

    zs = [jnp.where(col < qpos, _dot_nt(q2, kpad_ref[...].astype(BF16)) + bias, MASKED)]
    for p in reversed(range(n_pages)):
        zs.append(_dot(q2, k_refs[p][...].astype(BF16)) + bias)
    sp = jnp.concatenate([_split_bf16(_softplus2(z)) for z in zs], axis=0)
    r = _dot(sp, uu_ref[...])

    carry = jnp.zeros((rows, LANES), F32)
    acc = jnp.zeros((rows, SB_WIDTH), F32)
    for i, z in enumerate(zs):
        ri = r[i * rows:(i + 1) * rows]
        e = jnp.exp2(z - (ri[:, :LANES] + carry)).astype(BF16)
        carry = carry + ri[:, LANES:]
        if i == 0:
            acc = acc + _dot(e, vpad_ref[...].astype(BF16))
        else:
            acc = acc + _dot_nt(e, v_refs[n_pages - i][...].astype(BF16))

    own = jnp.where(own_head, acc, 0.0)
    o_ref[...] = jnp.sum(own.reshape(DEC_SEQ, N_HEADS, SB_WIDTH), axis=1)


def _attn_kernel(pt_ref, qp_ref, kp_ref, vp_ref, biasp_ref, uu_ref, qs_ref, kn_ref, vn_ref,
                 ck_ref, cv_ref, biass_ref, op_ref, os_ref,
                 q2_ref, z_ref, sp_ref, acc_ref, carry_ref, kpad_ref, vpad_ref, kbuf_ref, vbuf_ref, sem,
                 *, layer, n_pages):
    seq = pl.program_id(0) * pl.num_programs(1) + pl.program_id(1)
    n_seq = pl.num_programs(0) * pl.num_programs(1)
    slot = seq % 2

    def page_copies(s, into):
        copies = []
        for p in range(n_pages):
            page = pt_ref[s, p]
            copies.append(pltpu.make_async_copy(ck_ref.at[layer, page], kbuf_ref.at[into, p], sem.at[into]))
            copies.append(pltpu.make_async_copy(cv_ref.at[layer, page], vbuf_ref.at[into, p], sem.at[into]))
        return copies

    @pl.when(seq == 0)
    def _():
        for c in page_copies(0, 0):
            c.start()

    for this in (0, 1):
        @pl.when((slot == this) & (seq + 1 < n_seq))
        def _(this=this):
            for c in page_copies(seq + 1, 1 - this):
                c.start()

    _prompt_attention(pl.program_id(1), qp_ref, kp_ref, vp_ref, biasp_ref, uu_ref, op_ref,
                      q2_ref, z_ref, sp_ref, acc_ref, carry_ref)

    for this in (0, 1):
        @pl.when(slot == this)
        def _(this=this):
            for c in page_copies(seq, this):
                c.wait()
            _sample_attention(qs_ref, kn_ref, vn_ref,
                              [kbuf_ref.at[this, p] for p in range(n_pages)],
                              [vbuf_ref.at[this, p] for p in range(n_pages)],
                              biass_ref, uu_ref, os_ref, kpad_ref, vpad_ref)


def _attn_call(qb, kb, vb, bias_p, page_table, qs, k_new, v_new, cache_k, cache_v, layer, bias_s, uu):
    nb, t, _ = qb.shape
    blocks = t // CHUNK
    n_seq, n_pages = page_table.shape
    assert n_seq == nb * blocks, "one sample sequence rides along with each prompt query block"
    rows = DEC_SEQ * N_HEADS
    stacked = (N_PAIRS, 2 * CHUNK, LANES)
    slots = (2, n_pages, SB_WIDTH, CHUNK)
    new = lambda: pl.BlockSpec((None, DEC_SEQ, SB_WIDTH), lambda b, i, pt: (b * blocks + i, 0, 0))

    grid_spec = pltpu.PrefetchScalarGridSpec(
        num_scalar_prefetch=1,
        grid=(nb, blocks),
        in_specs=[
            pl.BlockSpec((None, CHUNK, SB_WIDTH), lambda b, i, pt: (b, i, 0)),
            pl.BlockSpec((None,) + kb.shape[1:], lambda b, i, pt: (b, 0, 0, 0)),
            pl.BlockSpec((None,) + vb.shape[1:], lambda b, i, pt: (b, 0, 0, 0)),
            pl.BlockSpec(stacked, lambda b, i, pt: (0, 0, 0)),
            pl.BlockSpec((2 * LANES, 2 * LANES), lambda b, i, pt: (0, 0)),
            new(), new(), new(),
            pl.BlockSpec(memory_space=pl.ANY), pl.BlockSpec(memory_space=pl.ANY),
            pl.BlockSpec((rows, LANES), lambda b, i, pt: (0, 0)),
        ],
        out_specs=[
            pl.BlockSpec((None, CHUNK, SB_WIDTH), lambda b, i, pt: (b, i, 0)),
            pl.BlockSpec((None, DEC_SEQ, SB_WIDTH), lambda b, i, pt: (b * blocks + i, 0, 0)),
        ],
        scratch_shapes=[
            pltpu.VMEM(stacked, BF16),
            pltpu.VMEM(stacked, F32),
            pltpu.VMEM(stacked, BF16),
            pltpu.VMEM(stacked, F32),
            pltpu.VMEM(stacked, F32),
            pltpu.VMEM((CHUNK, SB_WIDTH), F32),
            pltpu.VMEM((CHUNK, SB_WIDTH), F32),
            pltpu.VMEM(slots, F32),
            pltpu.VMEM(slots, F32),
            pltpu.SemaphoreType.DMA((2,)),
        ],
    )
    return pl.pallas_call(
        functools.partial(_attn_kernel, layer=layer, n_pages=n_pages),
        grid_spec=grid_spec,
        out_shape=[jax.ShapeDtypeStruct((nb, t, SB_WIDTH), F32),
                   jax.ShapeDtypeStruct((n_seq, DEC_SEQ, SB_WIDTH), F32)],
        compiler_params=pltpu.CompilerParams(
            dimension_semantics=("arbitrary", "arbitrary"), vmem_limit_bytes=VMEM_LIMIT),
        name="attn",
    )(page_table, qb, kb, vb, bias_p, uu, qs, k_new, v_new, cache_k, cache_v, bias_s)


def _gate_prompt(gvn, ws_ref, gbias):
    tm = gvn.shape[0]
    lane = lax.broadcasted_iota(jnp.int32, (CHUNK, LANES), 1)
    row = lax.broadcasted_iota(jnp.int32, (CHUNK, LANES), 0)
    first_group = lane < HEAD_DIM
    keep = lane <= row
    out_cols = []
    for gp in range(N_PAIRS):
        w_pair = jnp.concatenate(
            [jnp.where(keep, ws_ref[2 * gp], 0.0), jnp.where(keep, ws_ref[2 * gp + 1], 0.0)],
            axis=1).astype(BF16)
        chunks = []
        for c in range(tm // CHUNK):
            r = gvn[c * CHUNK:(c + 1) * CHUNK, gp * LANES:(gp + 1) * LANES]
            r2 = jnp.concatenate(
                [jnp.where(first_group, r, 0.0), jnp.where(first_group, 0.0, r)], axis=0).astype(BF16)
            chunks.append(_dot(w_pair, r2) + gbias[:, gp * LANES:(gp + 1) * LANES])
        out_cols.append(jnp.concatenate(chunks, axis=0))
    return jnp.concatenate(out_cols, axis=1)


def _gate_sample(gvn, pat_ref):
    tm = gvn.shape[0]
    g3 = gvn.reshape(tm // 8, 8, GM_WIDTH)
    out = g3 * pat_ref[0][None] + pat_ref[DEC_SEQ][None]
    for k in range(1, DEC_SEQ):
        out = out + pltpu.roll(g3, k, axis=1) * pat_ref[k][None]
    return out.reshape(tm, GM_WIDTH)


def _post_kernel(x_ref, ma_ref, mb_ref, osb_ref, u_ref, gvn_ref, gate_w_ref, gate_b_ref, gnsb_ref,
                 gngm_ref, wout_ref, gff_ref, w1_ref, w2_ref, gfin_ref, *out_refs, sample, final):
    gate1, shift2 = ma_ref[:, :D_MODEL], ma_ref[:, D_MODEL:]
    scale2, gate2 = mb_ref[:, :D_MODEL], mb_ref[:, D_MODEL:]
    x = x_ref[...]
    o_sb = _rms(osb_ref[...], gnsb_ref[...])
    if sample:
        mixed = _gate_sample(gvn_ref[...], gate_w_ref)
    else:
        mixed = _gate_prompt(gvn_ref[...], gate_w_ref, gate_b_ref[...])
    o_gm = _rms(u_ref[...] * mixed, gngm_ref[...])
    cat = jnp.concatenate([o_sb, o_gm], axis=-1).astype(BF16)
    x = x + gate1 * _dot(cat, wout_ref[...])
    h = (_rms(x, gff_ref[...]) * (1.0 + scale2) + shift2).astype(BF16)
    ff = None
    n_split = 4
    cw = D_FF // n_split
    for c in range(n_split):
        a = jnp.maximum(_dot(h, w1_ref[:, c * cw:(c + 1) * cw]), 0.0)
        part = _dot((a * a).astype(BF16), w2_ref[c * cw:(c + 1) * cw, :])
        ff = part if ff is None else ff + part
    x = x + gate2 * ff
    out_refs[0][...] = x
    if final:
        out_refs[1][...] = _rms(x, gfin_ref[...])


def _post_call(x, mods, mod_lead, layer, o_sb, u, gvn, gate_w, gate_b, gn_sb, gn_gm, w_out, g_ff,
               w_ff1, w_ff2, g_final, tm, sample, final):
    nb, t, _ = x.shape
    mod_rows = 1 if mod_lead is None else tm
    tok = lambda w: pl.BlockSpec((None, tm, w), lambda b, i: (b, i, 0))

    def const(shape):
        return pl.BlockSpec(shape, lambda b, i: (0,) * len(shape), pipeline_mode=pl.Buffered(1))

    n_out = 2 if final else 1
    outs = pl.pallas_call(
        functools.partial(_post_kernel, sample=sample, final=final),
        grid=(nb, t // tm),
        in_specs=[
            tok(D_MODEL),
            _mod_spec(mod_rows, 1, mod_lead), _mod_spec(mod_rows, 2, mod_lead),
            tok(SB_WIDTH), tok(GM_WIDTH), tok(GM_WIDTH),
            const(gate_w.shape), const(gate_b.shape),
            const((1, SB_WIDTH)), const((1, GM_WIDTH)),
            _layer_spec(w_out, layer),
            const((1, D_MODEL)),
            _layer_spec(w_ff1, layer), _layer_spec(w_ff2, layer),
            const((1, D_MODEL)),
        ],
        out_specs=[tok(D_MODEL)] * n_out,
        out_shape=[jax.ShapeDtypeStruct((nb, t, D_MODEL), F32)] * n_out,
        compiler_params=pltpu.CompilerParams(
            dimension_semantics=("arbitrary", "arbitrary"), vmem_limit_bytes=VMEM_LIMIT),
        name="post_sample" if sample else "post_prompt",
    )(x, mods, mods, o_sb, u, gvn, gate_w, gate_b, gn_sb.reshape(1, SB_WIDTH),
      gn_gm.reshape(1, GM_WIDTH), w_out, g_ff.reshape(1, D_MODEL), w_ff1, w_ff2,
      g_final.reshape(1, D_MODEL))
    return outs


def _suffix_sum_matrix():
    j = jnp.arange(LANES)[:, None]
    s = jnp.arange(LANES)[None, :]
    half = jnp.concatenate([(j >= s).astype(BF16), jnp.ones((LANES, LANES), BF16)], axis=1)
    return jnp.concatenate([half, half], axis=0)


def _sample_gate_pattern(w_s, b_s):
    w4 = w_s[:, :, :DEC_SEQ, :DEC_SEQ]
    k = jnp.arange(DEC_SEQ)[:, None, None]
    t = jnp.arange(DEC_SEQ)[None, :, None]
    s = jnp.arange(DEC_SEQ)[None, None, :]
    pick = (s == t - k).astype(F32)
    pat = jnp.sum(w4[:, None] * pick[None, :, None], axis=-1)
    pat = jnp.concatenate([pat, b_s[:, None, :, :DEC_SEQ]], axis=1)
    pat = jnp.swapaxes(pat, 2, 3)
    pat = jnp.concatenate([pat, pat], axis=2)
    return jnp.repeat(pat, HEAD_DIM, axis=-1)


def kernel(x_prompt, x_sample, cache_k, cache_v, page_table, c_prompt, c_sample, w_ada, b_ada, g_mix,
           w_in, sb_bias, gm_vnorm, w_s, b_s, gn_sb, gn_gm, w_out, g_ff, w_ff1, w_ff2, g_final):
    depth = w_in.shape[0]
    n_b, seq, _ = x_prompt.shape
    n_seq, dec_seq, _ = x_sample.shape
    n_pool = cache_k.shape[1]
    n_tok_s = n_seq * dec_seq

    mods = _ada_call(jnp.concatenate([jnp.repeat(c_sample, dec_seq, axis=0), c_prompt], axis=0),
                     w_ada, b_ada)
    uu = _suffix_sum_matrix()
    w_a = jnp.concatenate([w_in[:, :, :SB_WIDTH], w_in[:, :, 3 * SB_WIDTH:]], axis=2).astype(BF16)
    w_kv = w_in[:, :, SB_WIDTH:3 * SB_WIDTH].astype(BF16)
    w_kv_t = jnp.swapaxes(w_kv, 1, 2)
    w_out_b = w_out.astype(BF16)
    w1_b = w_ff1.astype(BF16)
    w2_b = w_ff2.astype(BF16)
    bias2 = sb_bias * LOG2_E
    bias_p_all = jnp.broadcast_to(jnp.repeat(bias2, CHUNK, axis=1).reshape(depth, N_PAIRS, 2 * CHUNK, 1),
                                  (depth, N_PAIRS, 2 * CHUNK, LANES))
    bias_s_all = jnp.broadcast_to(jnp.tile(bias2, (1, dec_seq))[:, :, None],
                                  (depth, dec_seq * N_HEADS, LANES))
    gate_b_all = jnp.repeat(jnp.swapaxes(b_s, 1, 2), HEAD_DIM, axis=2)
    gate_pat_all = _sample_gate_pattern(w_s, b_s)
    cache_k = jnp.transpose(cache_k, (0, 1, 3, 4, 2)).reshape(depth, n_pool, SB_WIDTH, CHUNK)
    cache_v = jnp.transpose(cache_v, (0, 1, 3, 4, 2)).reshape(depth, n_pool, SB_WIDTH, CHUNK)

    xp = x_prompt
    xs = x_sample.reshape(1, n_tok_s, D_MODEL)
    gp, ksm, vsm, gsm = [], [], [], []
    yp = ys = k_pages = v_pages = None
    for l in range(depth):
        final = l == depth - 1
        mods_p = mods[l, n_tok_s:][:, None, :]
        bias_p, bias_s, gate_b_p, gate_pat_s = bias_p_all[l], bias_s_all[l], gate_b_all[l], gate_pat_all[l]

        qb, u, gvn, k_pages, v_pages, kb, vb = _pre_call(
            xp, mods_p, None, l, g_mix[l], w_a, w_kv_t, gm_vnorm[l], 512,
            paged=(depth, k_pages, v_pages))
        qb_s, u_s, gvn_s, k_s, v_s, _, _ = _pre_call(
            xs, mods, l, l, g_mix[l], w_a, w_kv, gm_vnorm[l], n_tok_s)
        per_seq = lambda a: a.reshape(n_seq, dec_seq, SB_WIDTH)
        o_sb, o_sb_s = _attn_call(qb, kb, vb, bias_p, page_table, per_seq(qb_s), per_seq(k_s),
                                  per_seq(v_s), cache_k, cache_v, l, bias_s, uu)

        outs = _post_call(xp, mods_p, None, l, o_sb, u, gvn, w_s[l], gate_b_p, gn_sb[l], gn_gm[l],
                          w_out_b, g_ff[l], w1_b, w2_b, g_final, 512, False, final)
        xp = outs[0]
        if final:
            yp = outs[1]
        gp.append(gvn[:, seq - CHUNK:].reshape(n_b, CHUNK, N_HEADS, HEAD_DIM))

        outs = _post_call(xs, mods, l, l, o_sb_s.reshape(1, n_tok_s, SB_WIDTH), u_s, gvn_s, gate_pat_s,
                          gate_b_p, gn_sb[l], gn_gm[l], w_out_b, g_ff[l], w1_b, w2_b, g_final,
                          n_tok_s, True, final)
        xs = outs[0]
        if final:
            ys = outs[1]
        ksm.append(k_s.reshape(n_seq, dec_seq, N_HEADS, HEAD_DIM))
        vsm.append(v_s.reshape(n_seq, dec_seq, N_HEADS, HEAD_DIM))
        gsm.append(gvn_s.reshape(n_seq, dec_seq, N_HEADS, HEAD_DIM))

    def unpage(pages):
        p = pages.reshape(depth, n_b, seq // CHUNK, N_HEADS, HEAD_DIM, CHUNK)
        return jnp.transpose(p, (0, 1, 2, 5, 3, 4))

    return (yp, ys.reshape(n_seq, dec_seq, D_MODEL), unpage(k_pages), unpage(v_pages), jnp.stack(gp),
            jnp.stack(ksm), jnp.stack(vsm), jnp.stack(gsm))
```

```python
import functools

import jax
import jax.numpy as jnp
from jax import lax
from jax.experimental import pallas as pl
from jax.experimental.pallas import tpu as pltpu

D_MODEL = 1024
SB_WIDTH = 512
GM_WIDTH = 512
HEAD_DIM = 64
N_HEADS = 8
N_PAIRS = 4
IN_WIDTH = 3 * SB_WIDTH + 2 * GM_WIDTH
D_FF = 4 * D_MODEL
CHUNK = 128
DEC_SEQ = 4
EPS = 1e-6
LOG2_E = 1.4426950408889634
MASKED = -1e30
LANES = 128
UNROLLS = (8, 4, 2, 1)
VMEM_LIMIT = 56 * 1024 * 1024

F32 = jnp.float32
BF16 = jnp.bfloat16


def _rms(x, g):
    ms = jnp.mean(x * x, axis=-1, keepdims=True)
    return x * lax.rsqrt(ms + EPS) * g


def _softplus2(z):
    return jnp.maximum(z, 0.0) + jnp.log(1.0 + jnp.exp2(-jnp.abs(z))) * LOG2_E


def _split_bf16(x):
    hi = x.astype(BF16)
    lo = (x - hi.astype(F32)).astype(BF16)
    return jnp.concatenate([hi, lo], axis=1)


def _dot(a, b):
    return jnp.dot(a, b, preferred_element_type=F32)


def _dot_nt(a, b):
    return lax.dot_general(a, b, (((1,), (1,)), ((), ())), preferred_element_type=F32)


def _ada_kernel(c_ref, w_ref, b_ref, o_ref):
    c = c_ref[...]
    s = c * jax.nn.sigmoid(c)
    o_ref[0] = _dot(s.astype(BF16), w_ref[0].astype(BF16)) + b_ref[0]


def _ada_call(c_all, w_ada, b_ada):
    depth, _, n_out = w_ada.shape
    rows = c_all.shape[0]
    tn = 1024
    return pl.pallas_call(
        _ada_kernel,
        grid=(depth, n_out // tn),
        in_specs=[
            pl.BlockSpec((rows, D_MODEL), lambda l, j: (0, 0)),
            pl.BlockSpec((1, D_MODEL, tn), lambda l, j: (l, 0, j)),
            pl.BlockSpec((1, 1, tn), lambda l, j: (l, 0, j)),
        ],
        out_specs=pl.BlockSpec((1, rows, tn), lambda l, j: (l, 0, j)),
        out_shape=jax.ShapeDtypeStruct((depth, rows, n_out), F32),
        compiler_params=pltpu.CompilerParams(
            dimension_semantics=("arbitrary", "arbitrary"), vmem_limit_bytes=VMEM_LIMIT),
        name="ada",
    )(c_all, w_ada, b_ada.reshape(depth, 1, n_out))


def _pre_kernel(x_ref, m_ref, gmix_ref, wa_ref, wkv_ref, gvnorm_ref, *refs, pages):
    q_ref, u_ref, gvn_ref, k_ref, v_ref, kb_ref, vb_ref = refs[-7:]
    x = x_ref[...]
    shift, scale = m_ref[:, :D_MODEL], m_ref[:, D_MODEL:]
    h = (_rms(x, gmix_ref[...]) * (1.0 + scale) + shift).astype(BF16)
    pa = _dot(h, wa_ref[...])
    q_ref[...] = (pa[:, :SB_WIDTH] * (HEAD_DIM ** -0.5 * LOG2_E)).astype(BF16)
    u_ref[...] = pa[:, SB_WIDTH:SB_WIDTH + GM_WIDTH]
    gvn_ref[...] = _rms(pa[:, SB_WIDTH + GM_WIDTH:], gvnorm_ref[...])
    if pages:
        kv = _dot_nt(wkv_ref[...], h)
        for c in range(pages):
            blk = kv[:, c * CHUNK:(c + 1) * CHUNK]
            k_ref[c] = blk[:SB_WIDTH]
            v_ref[c] = blk[SB_WIDTH:]
            kb_ref[c] = blk[:SB_WIDTH].astype(BF16)
            vb_ref[c] = blk[SB_WIDTH:].astype(BF16)
    else:
        kv = _dot(h, wkv_ref[...])
        k_ref[...] = kv[:, :SB_WIDTH]
        v_ref[...] = kv[:, SB_WIDTH:]
        kb_ref[...] = kv[:, :SB_WIDTH].astype(BF16)
        vb_ref[...] = kv[:, SB_WIDTH:].astype(BF16)


def _mod_spec(rows, pair, lead):
    if lead is None:
        return pl.BlockSpec((None, rows, 2 * D_MODEL), lambda b, i: (b, 0, pair))
    return pl.BlockSpec((None, rows, 2 * D_MODEL), lambda b, i: (lead, 0, pair),
                        pipeline_mode=pl.Buffered(1))


def _layer_spec(w, layer):
    return pl.BlockSpec((None,) + w.shape[1:], lambda b, i: (layer,) + (0,) * (w.ndim - 1),
                        pipeline_mode=pl.Buffered(1))


def _pre_call(x, mods, mod_lead, layer, g_mix, w_a, w_kv, gm_vnorm, tm, paged=None):
    nb, t, _ = x.shape
    tok = lambda w: pl.BlockSpec((None, tm, w), lambda b, i: (b, i, 0))
    const = lambda shape: pl.BlockSpec(shape, lambda b, i: (0,) * len(shape))
    out_specs = [tok(SB_WIDTH), tok(GM_WIDTH), tok(GM_WIDTH)]
    out_shape = [jax.ShapeDtypeStruct((nb, t, SB_WIDTH), BF16),
                 jax.ShapeDtypeStruct((nb, t, GM_WIDTH), F32),
                 jax.ShapeDtypeStruct((nb, t, GM_WIDTH), F32)]
    operands = [x, mods, g_mix.reshape(1, D_MODEL), w_a, w_kv, gm_vnorm.reshape(1, GM_WIDTH)]
    in_specs = [tok(D_MODEL), _mod_spec(1 if mod_lead is None else tm, 0, mod_lead),
                const((1, D_MODEL)), _layer_spec(w_a, layer), _layer_spec(w_kv, layer),
                const((1, GM_WIDTH))]
    aliases = {}
    if paged is None:
        pages = 0
        out_specs += [tok(SB_WIDTH)] * 4
        out_shape += [jax.ShapeDtypeStruct((nb, t, SB_WIDTH), dt) for dt in (F32, F32, BF16, BF16)]
    else:
        depth, k_all, v_all = paged
        pages = tm // CHUNK
        n_pages = t // CHUNK
        page_shape = (pages, SB_WIDTH, CHUNK)
        out_specs += [pl.BlockSpec((None, None) + page_shape, lambda b, i: (layer, b, i, 0, 0))] * 2
        out_specs += [pl.BlockSpec((None,) + page_shape, lambda b, i: (b, i, 0, 0))] * 2
        out_shape += [jax.ShapeDtypeStruct((depth, nb, n_pages, SB_WIDTH, CHUNK), F32)] * 2
        out_shape += [jax.ShapeDtypeStruct((nb, n_pages, SB_WIDTH, CHUNK), BF16)] * 2
        if k_all is not None:
            aliases = {len(operands): 3, len(operands) + 1: 4}
            operands += [k_all, v_all]
            in_specs += [pl.BlockSpec(memory_space=pl.ANY)] * 2
    return pl.pallas_call(
        functools.partial(_pre_kernel, pages=pages),
        grid=(nb, t // tm),
        in_specs=in_specs,
        out_specs=out_specs,
        out_shape=out_shape,
        input_output_aliases=aliases,
        compiler_params=pltpu.CompilerParams(
            dimension_semantics=("arbitrary", "arbitrary"), vmem_limit_bytes=VMEM_LIMIT),
        name="pre",
    )(*operands)


def _prompt_attention(qi, q_ref, k_ref, v_ref, bias_ref, uu_ref, o_ref,
                      q2_ref, z_ref, sp_ref, acc_ref, carry_ref):
    lane = lax.broadcasted_iota(jnp.int32, (CHUNK, LANES), 1)
    row = lax.broadcasted_iota(jnp.int32, (CHUNK, LANES), 0)
    first_head = lane < HEAD_DIM
    causal = lane < row
    causal2 = jnp.concatenate([causal, causal], axis=0)
    cols = [slice(hp * LANES, (hp + 1) * LANES) for hp in range(N_PAIRS)]

    qf = q_ref[...].astype(F32)
    for hp in range(N_PAIRS):
        qp = qf[:, cols[hp]]
        q2_ref[hp] = jnp.concatenate(
            [jnp.where(first_head, qp, 0.0), jnp.where(first_head, 0.0, qp)], axis=0).astype(BF16)
        acc_ref[hp] = jnp.zeros((2 * CHUNK, LANES), F32)
        carry_ref[hp] = jnp.zeros((2 * CHUNK, LANES), F32)

    def scores(kb):
        ks = k_ref[kb]
        return [_dot(q2_ref[hp], ks[cols[hp], :]) + bias_ref[hp] for hp in range(N_PAIRS)]

    def stash(zs):
        for hp in range(N_PAIRS):
            z_ref[hp] = zs[hp]
            sp_ref[hp] = _softplus2(zs[hp]).astype(BF16)

    def finish(kb):
        vs = v_ref[kb]
        zs = [z_ref[hp] for hp in range(N_PAIRS)]
        rs = [_dot(sp_ref[hp], uu_ref[:LANES, :]) for hp in range(N_PAIRS)]
        zs_next = scores(jnp.maximum(kb - 1, 0))
        es = []
        for hp in range(N_PAIRS):
            carry = carry_ref[hp]
            es.append(jnp.exp2(zs[hp] - (rs[hp][:, :LANES] + carry)).astype(BF16))
            carry_ref[hp] = carry + rs[hp][:, LANES:]
        for hp in range(N_PAIRS):
            acc_ref[hp] += _dot_nt(es[hp], vs[cols[hp], :])
        stash(zs_next)

    stash([jnp.where(causal2, z, MASKED) for z in scores(qi)])

    top = qi
    for width in UNROLLS:
        def body(i, c, top=top, width=width):
            for j in range(width):
                finish(top - width * i - j)
            return c

        trips = (top + 1) // width
        lax.fori_loop(0, trips, body, 0)
        top = top - trips * width

    for hp in range(N_PAIRS):
        a = acc_ref[hp]
        o_ref[:, cols[hp]] = jnp.where(first_head, a[:CHUNK], a[CHUNK:])


def _sample_attention(q_ref, kn_ref, vn_ref, k_refs, v_refs, bias_ref, uu_ref, o_ref,
                      kpad_ref, vpad_ref):
    n_pages = len(k_refs)
    rows = DEC_SEQ * N_HEADS

    lane = lax.broadcasted_iota(jnp.int32, (rows, SB_WIDTH), 1)
    row = lax.broadcasted_iota(jnp.int32, (rows, SB_WIDTH), 0)
    own_head = lane // HEAD_DIM == row % N_HEADS
    qf = q_ref[...].astype(F32)
    qrep = jnp.broadcast_to(qf[:, None, :], (DEC_SEQ, N_HEADS, SB_WIDTH)).reshape(rows, SB_WIDTH)
    q2 = jnp.where(own_head, qrep, 0.0).astype(BF16)

    kpad_ref[0:DEC_SEQ, :] = kn_ref[...]
    vpad_ref[0:DEC_SEQ, :] = vn_ref[...]
    col = lax.broadcasted_iota(jnp.int32, (rows, LANES), 1)
    qpos = lax.broadcasted_iota(jnp.int32, (rows, LANES), 0) // N_HEADS
    bias = bias_ref[...]

    zs = [jnp.where(col < qpos, _dot_nt(q2, kpad_ref[...].astype(BF16)) + bias, MASKED)]
    for p in reversed(range(n_pages)):
        zs.append(_dot(q2, k_refs[p][...].astype(BF16)) + bias)
    sp = jnp.concatenate([_split_bf16(_softplus2(z)) for z in zs], axis=0)
    r = _dot(sp, uu_ref[...])

    carry = jnp.zeros((rows, LANES), F32)
    acc = jnp.zeros((rows, SB_WIDTH), F32)
    for i, z in enumerate(zs):
        ri = r[i * rows:(i + 1) * rows]
        e = jnp.exp2(z - (ri[:, :LANES] + carry)).astype(BF16)
        carry = carry + ri[:, LANES:]
        if i == 0:
            acc = acc + _dot(e, vpad_ref[...].astype(BF16))
        else:
            acc = acc + _dot_nt(e, v_refs[n_pages - i][...].astype(BF16))

    own = jnp.where(own_head, acc, 0.0)
    o_ref[...] = jnp.sum(own.reshape(DEC_SEQ, N_HEADS, SB_WIDTH), axis=1)


def _attn_kernel(pt_ref, qp_ref, kp_ref, vp_ref, biasp_ref, uu_ref, qs_ref, kn_ref, vn_ref,
                 ck_ref, cv_ref, biass_ref, op_ref, os_ref,
                 q2_ref, z_ref, sp_ref, acc_ref, carry_ref, kpad_ref, vpad_ref, kbuf_ref, vbuf_ref, sem,
                 *, layer, n_pages):
    seq = pl.program_id(0) * pl.num_programs(1) + pl.program_id(1)
    n_seq = pl.num_programs(0) * pl.num_programs(1)
    slot = seq % 2

    def page_copies(s, into):
        copies = []
        for p in range(n_pages):
            page = pt_ref[s, p]
            copies.append(pltpu.make_async_copy(ck_ref.at[layer, page], kbuf_ref.at[into, p], sem.at[into]))
            copies.append(pltpu.make_async_copy(cv_ref.at[layer, page], vbuf_ref.at[into, p], sem.at[into]))
        return copies

    @pl.when(seq == 0)
    def _():
        kpad_ref[...] = jnp.zeros_like(kpad_ref)
        vpad_ref[...] = jnp.zeros_like(vpad_ref)
        for c in page_copies(0, 0):
            c.start()

    ahead = jnp.minimum(seq + 1, n_seq - 1)
    for c in page_copies(ahead, 1 - slot):
        c.start()

    _prompt_attention(pl.program_id(1), qp_ref, kp_ref, vp_ref, biasp_ref, uu_ref, op_ref,
                      q2_ref, z_ref, sp_ref, acc_ref, carry_ref)

    @pl.when(seq == n_seq - 1)
    def _():
        for c in page_copies(ahead, 1 - slot):
            c.wait()

    for this in (0, 1):
        @pl.when(slot == this)
        def _(this=this):
            for c in page_copies(seq, this):
                c.wait()
            _sample_attention(qs_ref, kn_ref, vn_ref,
                              [kbuf_ref.at[this, p] for p in range(n_pages)],
                              [vbuf_ref.at[this, p] for p in range(n_pages)],
                              biass_ref, uu_ref, os_ref, kpad_ref, vpad_ref)


def _attn_call(qb, kb, vb, bias_p, page_table, qs, k_new, v_new, cache_k, cache_v, layer, bias_s, uu):
    nb, t, _ = qb.shape
    blocks = t // CHUNK
    n_seq, n_pages = page_table.shape
    assert n_seq == nb * blocks, "one sample sequence rides along with each prompt query block"
    rows = DEC_SEQ * N_HEADS
    stacked = (N_PAIRS, 2 * CHUNK, LANES)
    slots = (2, n_pages, SB_WIDTH, CHUNK)
    new = lambda: pl.BlockSpec((None, DEC_SEQ, SB_WIDTH), lambda b, i, pt: (b * blocks + i, 0, 0))

    grid_spec = pltpu.PrefetchScalarGridSpec(
        num_scalar_prefetch=1,
        grid=(nb, blocks),
        in_specs=[
            pl.BlockSpec((None, CHUNK, SB_WIDTH), lambda b, i, pt: (b, i, 0)),
            pl.BlockSpec((None,) + kb.shape[1:], lambda b, i, pt: (b, 0, 0, 0)),
            pl.BlockSpec((None,) + vb.shape[1:], lambda b, i, pt: (b, 0, 0, 0)),
            pl.BlockSpec(stacked, lambda b, i, pt: (0, 0, 0)),
            pl.BlockSpec((2 * LANES, 2 * LANES), lambda b, i, pt: (0, 0)),
            new(), new(), new(),
            pl.BlockSpec(memory_space=pl.ANY), pl.BlockSpec(memory_space=pl.ANY),
            pl.BlockSpec((rows, LANES), lambda b, i, pt: (0, 0)),
        ],
        out_specs=[
            pl.BlockSpec((None, CHUNK, SB_WIDTH), lambda b, i, pt: (b, i, 0)),
            pl.BlockSpec((None, DEC_SEQ, SB_WIDTH), lambda b, i, pt: (b * blocks + i, 0, 0)),
        ],
        scratch_shapes=[
            pltpu.VMEM(stacked, BF16),
            pltpu.VMEM(stacked, F32),
            pltpu.VMEM(stacked, BF16),
            pltpu.VMEM(stacked, F32),
            pltpu.VMEM(stacked, F32),
            pltpu.VMEM((CHUNK, SB_WIDTH), F32),
            pltpu.VMEM((CHUNK, SB_WIDTH), F32),
            pltpu.VMEM(slots, F32),
            pltpu.VMEM(slots, F32),
            pltpu.SemaphoreType.DMA((2,)),
        ],
    )
    return pl.pallas_call(
        functools.partial(_attn_kernel, layer=layer, n_pages=n_pages),
        grid_spec=grid_spec,
        out_shape=[jax.ShapeDtypeStruct((nb, t, SB_WIDTH), F32),
                   jax.ShapeDtypeStruct((n_seq, DEC_SEQ, SB_WIDTH), F32)],
        compiler_params=pltpu.CompilerParams(
            dimension_semantics=("arbitrary", "arbitrary"), vmem_limit_bytes=VMEM_LIMIT),
        name="attn",
    )(page_table, qb, kb, vb, bias_p, uu, qs, k_new, v_new, cache_k, cache_v, bias_s)


def _gate_prompt(gvn, ws_ref, gbias):
    tm = gvn.shape[0]
    lane = lax.broadcasted_iota(jnp.int32, (CHUNK, LANES), 1)
    row = lax.broadcasted_iota(jnp.int32, (CHUNK, LANES), 0)
    first_group = lane < HEAD_DIM
    keep = lane <= row
    out_cols = []
    for gp in range(N_PAIRS):
        w_pair = jnp.concatenate(
            [jnp.where(keep, ws_ref[2 * gp], 0.0), jnp.where(keep, ws_ref[2 * gp + 1], 0.0)],
            axis=1).astype(BF16)
        chunks = []
        for c in range(tm // CHUNK):
            r = gvn[c * CHUNK:(c + 1) * CHUNK, gp * LANES:(gp + 1) * LANES]
            r2 = jnp.concatenate(
                [jnp.where(first_group, r, 0.0), jnp.where(first_group, 0.0, r)], axis=0).astype(BF16)
            chunks.append(_dot(w_pair, r2) + gbias[:, gp * LANES:(gp + 1) * LANES])
        out_cols.append(jnp.concatenate(chunks, axis=0))
    return jnp.concatenate(out_cols, axis=1)


def _gate_sample(gvn, pat_ref):
    tm = gvn.shape[0]
    g3 = gvn.reshape(tm // 8, 8, GM_WIDTH)
    out = g3 * pat_ref[0][None] + pat_ref[DEC_SEQ][None]
    for k in range(1, DEC_SEQ):
        out = out + pltpu.roll(g3, k, axis=1) * pat_ref[k][None]
    return out.reshape(tm, GM_WIDTH)


def _post_kernel(x_ref, ma_ref, mb_ref, osb_ref, u_ref, gvn_ref, gate_w_ref, gate_b_ref, gnsb_ref,
                 gngm_ref, wout_ref, gff_ref, w1_ref, w2_ref, gfin_ref, *out_refs, sample, final):
    gate1, shift2 = ma_ref[:, :D_MODEL], ma_ref[:, D_MODEL:]
    scale2, gate2 = mb_ref[:, :D_MODEL], mb_ref[:, D_MODEL:]
    x = x_ref[...]
    o_sb = _rms(osb_ref[...], gnsb_ref[...])
    if sample:
        mixed = _gate_sample(gvn_ref[...], gate_w_ref)
    else:
        mixed = _gate_prompt(gvn_ref[...], gate_w_ref, gate_b_ref[...])
    o_gm = _rms(u_ref[...] * mixed, gngm_ref[...])
    cat = jnp.concatenate([o_sb, o_gm], axis=-1).astype(BF16)
    x = x + gate1 * _dot(cat, wout_ref[...])
    h = (_rms(x, gff_ref[...]) * (1.0 + scale2) + shift2).astype(BF16)
    ff = None
    n_split = 4
    cw = D_FF // n_split
    for c in range(n_split):
        a = jnp.maximum(_dot(h, w1_ref[:, c * cw:(c + 1) * cw]), 0.0)
        part = _dot((a * a).astype(BF16), w2_ref[c * cw:(c + 1) * cw, :])
        ff = part if ff is None else ff + part
    x = x + gate2 * ff
    out_refs[0][...] = x
    if final:
        out_refs[1][...] = _rms(x, gfin_ref[...])


def _post_call(x, mods, mod_lead, layer, o_sb, u, gvn, gate_w, gate_b, gn_sb, gn_gm, w_out, g_ff,
               w_ff1, w_ff2, g_final, tm, sample, final):
    nb, t, _ = x.shape
    mod_rows = 1 if mod_lead is None else tm
    tok = lambda w: pl.BlockSpec((None, tm, w), lambda b, i: (b, i, 0))

    def const(shape):
        return pl.BlockSpec(shape, lambda b, i: (0,) * len(shape), pipeline_mode=pl.Buffered(1))

    n_out = 2 if final else 1
    outs = pl.pallas_call(
        functools.partial(_post_kernel, sample=sample, final=final),
        grid=(nb, t // tm),
        in_specs=[
            tok(D_MODEL),
            _mod_spec(mod_rows, 1, mod_lead), _mod_spec(mod_rows, 2, mod_lead),
            tok(SB_WIDTH), tok(GM_WIDTH), tok(GM_WIDTH),
            const(gate_w.shape), const(gate_b.shape),
            const((1, SB_WIDTH)), const((1, GM_WIDTH)),
            _layer_spec(w_out, layer),
            const((1, D_MODEL)),
            _layer_spec(w_ff1, layer), _layer_spec(w_ff2, layer),
            const((1, D_MODEL)),
        ],
        out_specs=[tok(D_MODEL)] * n_out,
        out_shape=[jax.ShapeDtypeStruct((nb, t, D_MODEL), F32)] * n_out,
        compiler_params=pltpu.CompilerParams(
            dimension_semantics=("arbitrary", "arbitrary"), vmem_limit_bytes=VMEM_LIMIT),
        name="post_sample" if sample else "post_prompt",
    )(x, mods, mods, o_sb, u, gvn, gate_w, gate_b, gn_sb.reshape(1, SB_WIDTH),
      gn_gm.reshape(1, GM_WIDTH), w_out, g_ff.reshape(1, D_MODEL), w_ff1, w_ff2,
      g_final.reshape(1, D_MODEL))
    return outs


def _suffix_sum_matrix():
    j = jnp.arange(LANES)[:, None]
    s = jnp.arange(LANES)[None, :]
    half = jnp.concatenate([(j >= s).astype(BF16), jnp.ones((LANES, LANES), BF16)], axis=1)
    return jnp.concatenate([half, half], axis=0)


def _sample_gate_pattern(w_s, b_s):
    w4 = w_s[:, :, :DEC_SEQ, :DEC_SEQ]
    k = jnp.arange(DEC_SEQ)[:, None, None]
    t = jnp.arange(DEC_SEQ)[None, :, None]
    s = jnp.arange(DEC_SEQ)[None, None, :]
    pick = (s == t - k).astype(F32)
    pat = jnp.sum(w4[:, None] * pick[None, :, None], axis=-1)
    pat = jnp.concatenate([pat, b_s[:, None, :, :DEC_SEQ]], axis=1)
    pat = jnp.swapaxes(pat, 2, 3)
    pat = jnp.concatenate([pat, pat], axis=2)
    return jnp.repeat(pat, HEAD_DIM, axis=-1)


def kernel(x_prompt, x_sample, cache_k, cache_v, page_table, c_prompt, c_sample, w_ada, b_ada, g_mix,
           w_in, sb_bias, gm_vnorm, w_s, b_s, gn_sb, gn_gm, w_out, g_ff, w_ff1, w_ff2, g_final):
    depth = w_in.shape[0]
    n_b, seq, _ = x_prompt.shape
    n_seq, dec_seq, _ = x_sample.shape
    n_pool = cache_k.shape[1]
    n_tok_s = n_seq * dec_seq

    mods = _ada_call(jnp.concatenate([jnp.repeat(c_sample, dec_seq, axis=0), c_prompt], axis=0),
                     w_ada, b_ada)
    uu = _suffix_sum_matrix()
    w_a = jnp.concatenate([w_in[:, :, :SB_WIDTH], w_in[:, :, 3 * SB_WIDTH:]], axis=2).astype(BF16)
    w_kv = w_in[:, :, SB_WIDTH:3 * SB_WIDTH].astype(BF16)
    w_kv_t = jnp.swapaxes(w_kv, 1, 2)
    w_out_b = w_out.astype(BF16)
    w1_b = w_ff1.astype(BF16)
    w2_b = w_ff2.astype(BF16)
    bias2 = sb_bias * LOG2_E
    bias_p_all = jnp.broadcast_to(jnp.repeat(bias2, CHUNK, axis=1).reshape(depth, N_PAIRS, 2 * CHUNK, 1),
                                  (depth, N_PAIRS, 2 * CHUNK, LANES))
    bias_s_all = jnp.broadcast_to(jnp.tile(bias2, (1, dec_seq))[:, :, None],
                                  (depth, dec_seq * N_HEADS, LANES))
    gate_b_all = jnp.repeat(jnp.swapaxes(b_s, 1, 2), HEAD_DIM, axis=2)
    gate_pat_all = _sample_gate_pattern(w_s, b_s)
    cache_k = jnp.transpose(cache_k, (0, 1, 3, 4, 2)).reshape(depth, n_pool, SB_WIDTH, CHUNK)
    cache_v = jnp.transpose(cache_v, (0, 1, 3, 4, 2)).reshape(depth, n_pool, SB_WIDTH, CHUNK)

    xp = x_prompt
    xs = x_sample.reshape(1, n_tok_s, D_MODEL)
    gp, ksm, vsm, gsm = [], [], [], []
    yp = ys = k_pages = v_pages = None
    for l in range(depth):
        final = l == depth - 1
        mods_p = mods[l, n_tok_s:][:, None, :]
        bias_p, bias_s, gate_b_p, gate_pat_s = bias_p_all[l], bias_s_all[l], gate_b_all[l], gate_pat_all[l]

        qb, u, gvn, k_pages, v_pages, kb, vb = _pre_call(
            xp, mods_p, None, l, g_mix[l], w_a, w_kv_t, gm_vnorm[l], 512,
            paged=(depth, k_pages, v_pages))
        qb_s, u_s, gvn_s, k_s, v_s, _, _ = _pre_call(
            xs, mods, l, l, g_mix[l], w_a, w_kv, gm_vnorm[l], n_tok_s)
        per_seq = lambda a: a.reshape(n_seq, dec_seq, SB_WIDTH)
        o_sb, o_sb_s = _attn_call(qb, kb, vb, bias_p, page_table, per_seq(qb_s), per_seq(k_s),
                                  per_seq(v_s), cache_k, cache_v, l, bias_s, uu)

        outs = _post_call(xp, mods_p, None, l, o_sb, u, gvn, w_s[l], gate_b_p, gn_sb[l], gn_gm[l],
                          w_out_b, g_ff[l], w1_b, w2_b, g_final, 512, False, final)
        xp = outs[0]
        if final:
            yp = outs[1]
        gp.append(gvn[:, seq - CHUNK:].reshape(n_b, CHUNK, N_HEADS, HEAD_DIM))

        outs = _post_call(xs, mods, l, l, o_sb_s.reshape(1, n_tok_s, SB_WIDTH), u_s, gvn_s, gate_pat_s,
                          gate_b_p, gn_sb[l], gn_gm[l], w_out_b, g_ff[l], w1_b, w2_b, g_final,
                          n_tok_s, True, final)
        xs = outs[0]
        if final:
            ys = outs[1]
        ksm.append(k_s.reshape(n_seq, dec_seq, N_HEADS, HEAD_DIM))
        vsm.append(v_s.reshape(n_seq, dec_seq, N_HEADS, HEAD_DIM))
        gsm.append(gvn_s.reshape(n_seq, dec_seq, N_HEADS, HEAD_DIM))

    def unpage(pages):
        p = pages.reshape(depth, n_b, seq // CHUNK, N_HEADS, HEAD_DIM, CHUNK)
        return jnp.transpose(p, (0, 1, 2, 5, 3, 4))

    return (yp, ys.reshape(n_seq, dec_seq, D_MODEL), unpage(k_pages), unpage(v_pages), jnp.stack(gp),
            jnp.stack(ksm), jnp.stack(vsm), jnp.stack(gsm))
```

```python
import functools

import jax
import jax.numpy as jnp
from jax import lax
from jax.experimental import pallas as pl
from jax.experimental.pallas import tpu as pltpu

D_MODEL = 1024
SB_WIDTH = 512
GM_WIDTH = 512
HEAD_DIM = 64
N_HEADS = 8
N_PAIRS = 4
IN_WIDTH = 3 * SB_WIDTH + 2 * GM_WIDTH
D_FF = 4 * D_MODEL
CHUNK = 128
DEC_SEQ = 4
EPS = 1e-6
LOG2_E = 1.4426950408889634
MASKED = -1e30
LANES = 128
UNROLLS = (8, 4, 2, 1)
VMEM_LIMIT = 56 * 1024 * 1024

F32 = jnp.float32
BF16 = jnp.bfloat16


def _rms(x, g):
    ms = jnp.mean(x * x, axis=-1, keepdims=True)
    return x * lax.rsqrt(ms + EPS) * g


def _softplus2(z):
    return jnp.maximum(z, 0.0) + jnp.log(1.0 + jnp.exp2(-jnp.abs(z))) * LOG2_E


def _split_bf16(x):
    hi = x.astype(BF16)
    lo = (x - hi.astype(F32)).astype(BF16)
    return jnp.concatenate([hi, lo], axis=1)


def _dot(a, b):
    return jnp.dot(a, b, preferred_element_type=F32)


def _dot_nt(a, b):
    return lax.dot_general(a, b, (((1,), (1,)), ((), ())), preferred_element_type=F32)


def _ada_kernel(c_ref, w_ref, b_ref, o_ref):
    c = c_ref[...]
    s = c * jax.nn.sigmoid(c)
    o_ref[0] = _dot(s.astype(BF16), w_ref[0].astype(BF16)) + b_ref[0]


def _ada_call(c_all, w_ada, b_ada):
    depth, _, n_out = w_ada.shape
    rows = c_all.shape[0]
    tn = 2048
    return pl.pallas_call(
        _ada_kernel,
        grid=(depth, n_out // tn),
        in_specs=[
            pl.BlockSpec((rows, D_MODEL), lambda l, j: (0, 0)),
            pl.BlockSpec((1, D_MODEL, tn), lambda l, j: (l, 0, j)),
            pl.BlockSpec((1, 1, tn), lambda l, j: (l, 0, j)),
        ],
        out_specs=pl.BlockSpec((1, rows, tn), lambda l, j: (l, 0, j)),
        out_shape=jax.ShapeDtypeStruct((depth, rows, n_out), F32),
        compiler_params=pltpu.CompilerParams(
            dimension_semantics=("arbitrary", "arbitrary"), vmem_limit_bytes=VMEM_LIMIT),
        name="ada",
    )(c_all, w_ada, b_ada.reshape(depth, 1, n_out))


def _pre_kernel(x_ref, m_ref, gmix_ref, wa_ref, wkv_ref, gvnorm_ref, *refs, pages):
    q_ref, u_ref, gvn_ref, k_ref, v_ref, kb_ref, vb_ref = refs[-7:]
    x = x_ref[...]
    shift, scale = m_ref[:, :D_MODEL], m_ref[:, D_MODEL:]
    h = (_rms(x, gmix_ref[...]) * (1.0 + scale) + shift).astype(BF16)
    pa = _dot(h, wa_ref[...])
    q_ref[...] = (pa[:, :SB_WIDTH] * (HEAD_DIM ** -0.5 * LOG2_E)).astype(q_ref.dtype)
    u_ref[...] = pa[:, SB_WIDTH:SB_WIDTH + GM_WIDTH]
    gvn_ref[...] = _rms(pa[:, SB_WIDTH + GM_WIDTH:], gvnorm_ref[...])
    if pages:
        kv = _dot_nt(wkv_ref[...], h)
        for c in range(pages):
            blk = kv[:, c * CHUNK:(c + 1) * CHUNK]
            k_ref[c] = blk[:SB_WIDTH]
            v_ref[c] = blk[SB_WIDTH:]
            kb_ref[c] = blk[:SB_WIDTH].astype(BF16)
            vb_ref[c] = blk[SB_WIDTH:].astype(BF16)
    else:
        kv = _dot(h, wkv_ref[...])
        k_ref[...] = kv[:, :SB_WIDTH]
        v_ref[...] = kv[:, SB_WIDTH:]
        kb_ref[...] = kv[:, :SB_WIDTH].astype(BF16)
        vb_ref[...] = kv[:, SB_WIDTH:].astype(BF16)


def _mod_spec(rows, pair, lead):
    if lead is None:
        return pl.BlockSpec((None, rows, 2 * D_MODEL), lambda b, i: (b, 0, pair))
    return pl.BlockSpec((None, rows, 2 * D_MODEL), lambda b, i: (lead, 0, pair),
                        pipeline_mode=pl.Buffered(1))


def _layer_spec(w, layer):
    return pl.BlockSpec((None,) + w.shape[1:], lambda b, i: (layer,) + (0,) * (w.ndim - 1),
                        pipeline_mode=pl.Buffered(1))


def _pre_call(x, mods, mod_lead, layer, g_mix, w_a, w_kv, gm_vnorm, tm, paged=None):
    nb, t, _ = x.shape
    tok = lambda w: pl.BlockSpec((None, tm, w), lambda b, i: (b, i, 0))
    const = lambda shape: pl.BlockSpec(shape, lambda b, i: (0,) * len(shape))
    out_specs = [tok(SB_WIDTH), tok(GM_WIDTH), tok(GM_WIDTH)]
    out_shape = [jax.ShapeDtypeStruct((nb, t, SB_WIDTH), F32 if paged is None else BF16),
                 jax.ShapeDtypeStruct((nb, t, GM_WIDTH), F32),
                 jax.ShapeDtypeStruct((nb, t, GM_WIDTH), F32)]
    operands = [x, mods, g_mix.reshape(1, D_MODEL), w_a, w_kv, gm_vnorm.reshape(1, GM_WIDTH)]
    in_specs = [tok(D_MODEL), _mod_spec(1 if mod_lead is None else tm, 0, mod_lead),
                const((1, D_MODEL)), _layer_spec(w_a, layer), _layer_spec(w_kv, layer),
                const((1, GM_WIDTH))]
    aliases = {}
    if paged is None:
        pages = 0
        out_specs += [tok(SB_WIDTH)] * 4
        out_shape += [jax.ShapeDtypeStruct((nb, t, SB_WIDTH), dt) for dt in (F32, F32, BF16, BF16)]
    else:
        depth, k_all, v_all = paged
        pages = tm // CHUNK
        n_pages = t // CHUNK
        page_shape = (pages, SB_WIDTH, CHUNK)
        out_specs += [pl.BlockSpec((None, None) + page_shape, lambda b, i: (layer, b, i, 0, 0))] * 2
        out_specs += [pl.BlockSpec((None,) + page_shape, lambda b, i: (b, i, 0, 0))] * 2
        out_shape += [jax.ShapeDtypeStruct((depth, nb, n_pages, SB_WIDTH, CHUNK), F32)] * 2
        out_shape += [jax.ShapeDtypeStruct((nb, n_pages, SB_WIDTH, CHUNK), BF16)] * 2
        if k_all is not None:
            aliases = {len(operands): 3, len(operands) + 1: 4}
            operands += [k_all, v_all]
            in_specs += [pl.BlockSpec(memory_space=pl.ANY)] * 2
    return pl.pallas_call(
        functools.partial(_pre_kernel, pages=pages),
        grid=(nb, t // tm),
        in_specs=in_specs,
        out_specs=out_specs,
        out_shape=out_shape,
        input_output_aliases=aliases,
        compiler_params=pltpu.CompilerParams(
            dimension_semantics=("arbitrary", "arbitrary"), vmem_limit_bytes=VMEM_LIMIT),
        name="pre",
    )(*operands)


def _prompt_attention(qi, q_ref, k_ref, v_ref, bias_ref, uu_ref, o_ref,
                      q2_ref, z_ref, sp_ref, acc_ref, carry_ref):
    lane = lax.broadcasted_iota(jnp.int32, (CHUNK, LANES), 1)
    row = lax.broadcasted_iota(jnp.int32, (CHUNK, LANES), 0)
    first_head = lane < HEAD_DIM
    causal = lane < row
    causal2 = jnp.concatenate([causal, causal], axis=0)
    cols = [slice(hp * LANES, (hp + 1) * LANES) for hp in range(N_PAIRS)]

    qf = q_ref[...].astype(F32)
    for hp in range(N_PAIRS):
        qp = qf[:, cols[hp]]
        q2_ref[hp] = jnp.concatenate(
            [jnp.where(first_head, qp, 0.0), jnp.where(first_head, 0.0, qp)], axis=0).astype(BF16)
        acc_ref[hp] = jnp.zeros((2 * CHUNK, LANES), F32)
        carry_ref[hp] = jnp.zeros((2 * CHUNK, LANES), F32)

    def scores(kb):
        ks = k_ref[kb]
        return [_dot(q2_ref[hp], ks[cols[hp], :]) + bias_ref[hp] for hp in range(N_PAIRS)]

    def stash(zs):
        for hp in range(N_PAIRS):
            z_ref[hp] = zs[hp]
            sp_ref[hp] = _softplus2(zs[hp]).astype(BF16)

    def finish(kb):
        vs = v_ref[kb]
        zs = [z_ref[hp] for hp in range(N_PAIRS)]
        rs = [_dot(sp_ref[hp], uu_ref[:LANES, :]) for hp in range(N_PAIRS)]
        zs_next = scores(jnp.maximum(kb - 1, 0))
        es = []
        for hp in range(N_PAIRS):
            carry = carry_ref[hp]
            es.append(jnp.exp2(zs[hp] - (rs[hp][:, :LANES] + carry)).astype(BF16))
            carry_ref[hp] = carry + rs[hp][:, LANES:]
        for hp in range(N_PAIRS):
            acc_ref[hp] += _dot_nt(es[hp], vs[cols[hp], :])
        stash(zs_next)

    stash([jnp.where(causal2, z, MASKED) for z in scores(qi)])

    top = qi
    for width in UNROLLS:
        def body(i, c, top=top, width=width):
            for j in range(width):
                finish(top - width * i - j)
            return c

        trips = (top + 1) // width
        lax.fori_loop(0, trips, body, 0)
        top = top - trips * width

    for hp in range(N_PAIRS):
        a = acc_ref[hp]
        o_ref[:, cols[hp]] = jnp.where(first_head, a[:CHUNK], a[CHUNK:])


def _sample_attention(pair, half, q_ref, kn_ref, vn_ref, k_refs, v_refs, bias_ref, uu_ref, o_ref,
                      kpad_ref, vpad_ref):
    n_pages = len(k_refs)
    rows = DEC_SEQ * N_HEADS
    mine = slice(half * DEC_SEQ, (half + 1) * DEC_SEQ)

    lane = lax.broadcasted_iota(jnp.int32, (rows, SB_WIDTH), 1)
    row = lax.broadcasted_iota(jnp.int32, (rows, SB_WIDTH), 0)
    own_head = lane // HEAD_DIM == row % N_HEADS
    qf = q_ref[pair, :][mine]
    qrep = jnp.broadcast_to(qf[:, None, :], (DEC_SEQ, N_HEADS, SB_WIDTH)).reshape(rows, SB_WIDTH)
    q2 = jnp.where(own_head, qrep, 0.0).astype(BF16)

    kpad_ref[0:DEC_SEQ, :] = kn_ref[pair, :][mine]
    vpad_ref[0:DEC_SEQ, :] = vn_ref[pair, :][mine]
    col = lax.broadcasted_iota(jnp.int32, (rows, LANES), 1)
    qpos = lax.broadcasted_iota(jnp.int32, (rows, LANES), 0) // N_HEADS
    bias = bias_ref[...]

    zs = [jnp.where(col < qpos, _dot_nt(q2, kpad_ref[...].astype(BF16)) + bias, MASKED)]
    for p in reversed(range(n_pages)):
        zs.append(_dot(q2, k_refs[p][...].astype(BF16)) + bias)
    sp = jnp.concatenate([_split_bf16(_softplus2(z)) for z in zs], axis=0)
    r = _dot(sp, uu_ref[...])

    carry = jnp.zeros((rows, LANES), F32)
    acc = jnp.zeros((rows, SB_WIDTH), F32)
    for i, z in enumerate(zs):
        ri = r[i * rows:(i + 1) * rows]
        e = jnp.exp2(z - (ri[:, :LANES] + carry)).astype(BF16)
        carry = carry + ri[:, LANES:]
        if i == 0:
            acc = acc + _dot(e, vpad_ref[...].astype(BF16))
        else:
            acc = acc + _dot_nt(e, v_refs[n_pages - i][...].astype(BF16))

    own = jnp.where(own_head, acc, 0.0)
    out = jnp.sum(jnp.concatenate([own, own], axis=0).reshape(2 * DEC_SEQ, N_HEADS, SB_WIDTH), axis=1)
    tile_row = lax.broadcasted_iota(jnp.int32, (2 * DEC_SEQ, SB_WIDTH), 0)
    other = 0.0 if half == 0 else o_ref[pair, :]
    o_ref[pair, :] = jnp.where(tile_row // DEC_SEQ == half, out, other)


def _attn_kernel(pt_ref, qp_ref, kp_ref, vp_ref, biasp_ref, uu_ref, qs_ref, kn_ref, vn_ref,
                 ck_ref, cv_ref, biass_ref, op_ref, os_ref,
                 q2_ref, z_ref, sp_ref, acc_ref, carry_ref, kpad_ref, vpad_ref, kbuf_ref, vbuf_ref, sem,
                 *, layer, n_pages):
    seq = pl.program_id(0) * pl.num_programs(1) + pl.program_id(1)
    n_seq = pl.num_programs(0) * pl.num_programs(1)
    slot = seq % 2

    def page_copies(s, into):
        copies = []
        for p in range(n_pages):
            page = pt_ref[s, p]
            copies.append(pltpu.make_async_copy(ck_ref.at[layer, page], kbuf_ref.at[into, p], sem.at[into]))
            copies.append(pltpu.make_async_copy(cv_ref.at[layer, page], vbuf_ref.at[into, p], sem.at[into]))
        return copies

    @pl.when(seq == 0)
    def _():
        kpad_ref[...] = jnp.zeros_like(kpad_ref)
        vpad_ref[...] = jnp.zeros_like(vpad_ref)
        for c in page_copies(0, 0):
            c.start()

    ahead = jnp.minimum(seq + 1, n_seq - 1)
    for c in page_copies(ahead, 1 - slot):
        c.start()

    _prompt_attention(pl.program_id(1), qp_ref, kp_ref, vp_ref, biasp_ref, uu_ref, op_ref,
                      q2_ref, z_ref, sp_ref, acc_ref, carry_ref)

    @pl.when(seq == n_seq - 1)
    def _():
        for c in page_copies(ahead, 1 - slot):
            c.wait()

    for this in (0, 1):
        @pl.when(slot == this)
        def _(this=this):
            for c in page_copies(seq, this):
                c.wait()
            pair = pl.ds(pl.multiple_of(seq // 2 * (2 * DEC_SEQ), 2 * DEC_SEQ), 2 * DEC_SEQ)
            _sample_attention(pair, this, qs_ref, kn_ref, vn_ref,
                              [kbuf_ref.at[this, p] for p in range(n_pages)],
                              [vbuf_ref.at[this, p] for p in range(n_pages)],
                              biass_ref, uu_ref, os_ref, kpad_ref, vpad_ref)


def _attn_call(qb, kb, vb, bias_p, page_table, qs, k_new, v_new, cache_k, cache_v, layer, bias_s, uu):
    nb, t, _ = qb.shape
    blocks = t // CHUNK
    n_seq, n_pages = page_table.shape
    assert n_seq == nb * blocks, "one sample sequence rides along with each prompt query block"
    rows = DEC_SEQ * N_HEADS
    stacked = (N_PAIRS, 2 * CHUNK, LANES)
    slots = (2, n_pages, SB_WIDTH, CHUNK)
    new = lambda: pl.BlockSpec((n_seq * DEC_SEQ, SB_WIDTH), lambda b, i, pt: (0, 0))

    grid_spec = pltpu.PrefetchScalarGridSpec(
        num_scalar_prefetch=1,
        grid=(nb, blocks),
        in_specs=[
            pl.BlockSpec((None, CHUNK, SB_WIDTH), lambda b, i, pt: (b, i, 0)),
            pl.BlockSpec((None,) + kb.shape[1:], lambda b, i, pt: (b, 0, 0, 0)),
            pl.BlockSpec((None,) + vb.shape[1:], lambda b, i, pt: (b, 0, 0, 0)),
            pl.BlockSpec(stacked, lambda b, i, pt: (0, 0, 0)),
            pl.BlockSpec((2 * LANES, 2 * LANES), lambda b, i, pt: (0, 0)),
            new(), new(), new(),
            pl.BlockSpec(memory_space=pl.ANY), pl.BlockSpec(memory_space=pl.ANY),
            pl.BlockSpec((rows, LANES), lambda b, i, pt: (0, 0)),
        ],
        out_specs=[
            pl.BlockSpec((None, CHUNK, SB_WIDTH), lambda b, i, pt: (b, i, 0)),
            new(),
        ],
        scratch_shapes=[
            pltpu.VMEM(stacked, BF16),
            pltpu.VMEM(stacked, F32),
            pltpu.VMEM(stacked, BF16),
            pltpu.VMEM(stacked, F32),
            pltpu.VMEM(stacked, F32),
            pltpu.VMEM((CHUNK, SB_WIDTH), F32),
            pltpu.VMEM((CHUNK, SB_WIDTH), F32),
            pltpu.VMEM(slots, F32),
            pltpu.VMEM(slots, F32),
            pltpu.SemaphoreType.DMA((2,)),
        ],
    )
    return pl.pallas_call(
        functools.partial(_attn_kernel, layer=layer, n_pages=n_pages),
        grid_spec=grid_spec,
        out_shape=[jax.ShapeDtypeStruct((nb, t, SB_WIDTH), F32),
                   jax.ShapeDtypeStruct((n_seq * DEC_SEQ, SB_WIDTH), F32)],
        compiler_params=pltpu.CompilerParams(
            dimension_semantics=("arbitrary", "arbitrary"), vmem_limit_bytes=VMEM_LIMIT),
        name="attn",
    )(page_table, qb, kb, vb, bias_p, uu, qs, k_new, v_new, cache_k, cache_v, bias_s)


def _gate_prompt(gvn, ws_ref, gbias):
    tm = gvn.shape[0]
    lane = lax.broadcasted_iota(jnp.int32, (CHUNK, LANES), 1)
    row = lax.broadcasted_iota(jnp.int32, (CHUNK, LANES), 0)
    first_group = lane < HEAD_DIM
    keep = lane <= row
    out_cols = []
    for gp in range(N_PAIRS):
        w_pair = jnp.concatenate(
            [jnp.where(keep, ws_ref[2 * gp], 0.0), jnp.where(keep, ws_ref[2 * gp + 1], 0.0)],
            axis=1).astype(BF16)
        chunks = []
        for c in range(tm // CHUNK):
            r = gvn[c * CHUNK:(c + 1) * CHUNK, gp * LANES:(gp + 1) * LANES]
            r2 = jnp.concatenate(
                [jnp.where(first_group, r, 0.0), jnp.where(first_group, 0.0, r)], axis=0).astype(BF16)
            chunks.append(_dot(w_pair, r2) + gbias[:, gp * LANES:(gp + 1) * LANES])
        out_cols.append(jnp.concatenate(chunks, axis=0))
    return jnp.concatenate(out_cols, axis=1)


def _gate_sample(gvn, pat_ref):
    tm = gvn.shape[0]
    g3 = gvn.reshape(tm // 8, 8, GM_WIDTH)
    out = g3 * pat_ref[0][None] + pat_ref[DEC_SEQ][None]
    for k in range(1, DEC_SEQ):
        out = out + pltpu.roll(g3, k, axis=1) * pat_ref[k][None]
    return out.reshape(tm, GM_WIDTH)


def _post_kernel(x_ref, ma_ref, mb_ref, osb_ref, u_ref, gvn_ref, gate_w_ref, gate_b_ref, gnsb_ref,
                 gngm_ref, wout_ref, gff_ref, w1_ref, w2_ref, gfin_ref, *out_refs, sample, final):
    gate1, shift2 = ma_ref[:, :D_MODEL], ma_ref[:, D_MODEL:]
    scale2, gate2 = mb_ref[:, :D_MODEL], mb_ref[:, D_MODEL:]
    x = x_ref[...]
    o_sb = _rms(osb_ref[...], gnsb_ref[...])
    if sample:
        mixed = _gate_sample(gvn_ref[...], gate_w_ref)
    else:
        mixed = _gate_prompt(gvn_ref[...], gate_w_ref, gate_b_ref[...])
    o_gm = _rms(u_ref[...] * mixed, gngm_ref[...])
    cat = jnp.concatenate([o_sb, o_gm], axis=-1).astype(BF16)
    x = x + gate1 * _dot(cat, wout_ref[...])
    h = (_rms(x, gff_ref[...]) * (1.0 + scale2) + shift2).astype(BF16)
    ff = None
    n_split = 4
    cw = D_FF // n_split
    for c in range(n_split):
        a = jnp.maximum(_dot(h, w1_ref[:, c * cw:(c + 1) * cw]), 0.0)
        part = _dot((a * a).astype(BF16), w2_ref[c * cw:(c + 1) * cw, :])
        ff = part if ff is None else ff + part
    x = x + gate2 * ff
    out_refs[0][...] = x
    if final:
        out_refs[1][...] = _rms(x, gfin_ref[...])


def _post_call(x, mods, mod_lead, layer, o_sb, u, gvn, gate_w, gate_b, gn_sb, gn_gm, w_out, g_ff,
               w_ff1, w_ff2, g_final, tm, sample, final):
    nb, t, _ = x.shape
    mod_rows = 1 if mod_lead is None else tm
    tok = lambda w: pl.BlockSpec((None, tm, w), lambda b, i: (b, i, 0))

    def const(shape):
        return pl.BlockSpec(shape, lambda b, i: (0,) * len(shape), pipeline_mode=pl.Buffered(1))

    n_out = 2 if final else 1
    outs = pl.pallas_call(
        functools.partial(_post_kernel, sample=sample, final=final),
        grid=(nb, t // tm),
        in_specs=[
            tok(D_MODEL),
            _mod_spec(mod_rows, 1, mod_lead), _mod_spec(mod_rows, 2, mod_lead),
            tok(SB_WIDTH), tok(GM_WIDTH), tok(GM_WIDTH),
            const(gate_w.shape), const(gate_b.shape),
            const((1, SB_WIDTH)), const((1, GM_WIDTH)),
            _layer_spec(w_out, layer),
            const((1, D_MODEL)),
            _layer_spec(w_ff1, layer), _layer_spec(w_ff2, layer),
            const((1, D_MODEL)),
        ],
        out_specs=[tok(D_MODEL)] * n_out,
        out_shape=[jax.ShapeDtypeStruct((nb, t, D_MODEL), F32)] * n_out,
        compiler_params=pltpu.CompilerParams(
            dimension_semantics=("arbitrary", "arbitrary"), vmem_limit_bytes=VMEM_LIMIT),
        name="post_sample" if sample else "post_prompt",
    )(x, mods, mods, o_sb, u, gvn, gate_w, gate_b, gn_sb.reshape(1, SB_WIDTH),
      gn_gm.reshape(1, GM_WIDTH), w_out, g_ff.reshape(1, D_MODEL), w_ff1, w_ff2,
      g_final.reshape(1, D_MODEL))
    return outs


def _suffix_sum_matrix():
    j = jnp.arange(LANES)[:, None]
    s = jnp.arange(LANES)[None, :]
    half = jnp.concatenate([(j >= s).astype(BF16), jnp.ones((LANES, LANES), BF16)], axis=1)
    return jnp.concatenate([half, half], axis=0)


def _sample_gate_pattern(w_s, b_s):
    w4 = w_s[:, :, :DEC_SEQ, :DEC_SEQ]
    k = jnp.arange(DEC_SEQ)[:, None, None]
    t = jnp.arange(DEC_SEQ)[None, :, None]
    s = jnp.arange(DEC_SEQ)[None, None, :]
    pick = (s == t - k).astype(F32)
    pat = jnp.sum(w4[:, None] * pick[None, :, None], axis=-1)
    pat = jnp.concatenate([pat, b_s[:, None, :, :DEC_SEQ]], axis=1)
    pat = jnp.swapaxes(pat, 2, 3)
    pat = jnp.concatenate([pat, pat], axis=2)
    return jnp.repeat(pat, HEAD_DIM, axis=-1)


def kernel(x_prompt, x_sample, cache_k, cache_v, page_table, c_prompt, c_sample, w_ada, b_ada, g_mix,
           w_in, sb_bias, gm_vnorm, w_s, b_s, gn_sb, gn_gm, w_out, g_ff, w_ff1, w_ff2, g_final):
    depth = w_in.shape[0]
    n_b, seq, _ = x_prompt.shape
    n_seq, dec_seq, _ = x_sample.shape
    n_pool = cache_k.shape[1]
    n_tok_s = n_seq * dec_seq

    mods = _ada_call(jnp.concatenate([jnp.repeat(c_sample, dec_seq, axis=0), c_prompt], axis=0),
                     w_ada, b_ada)
    uu = _suffix_sum_matrix()
    w_a = jnp.concatenate([w_in[:, :, :SB_WIDTH], w_in[:, :, 3 * SB_WIDTH:]], axis=2).astype(BF16)
    w_kv = w_in[:, :, SB_WIDTH:3 * SB_WIDTH].astype(BF16)
    w_kv_t = jnp.swapaxes(w_kv, 1, 2)
    w_out_b = w_out.astype(BF16)
    w1_b = w_ff1.astype(BF16)
    w2_b = w_ff2.astype(BF16)
    bias2 = sb_bias * LOG2_E
    bias_p_all = jnp.broadcast_to(jnp.repeat(bias2, CHUNK, axis=1).reshape(depth, N_PAIRS, 2 * CHUNK, 1),
                                  (depth, N_PAIRS, 2 * CHUNK, LANES))
    bias_s_all = jnp.broadcast_to(jnp.tile(bias2, (1, dec_seq))[:, :, None],
                                  (depth, dec_seq * N_HEADS, LANES))
    gate_b_all = jnp.repeat(jnp.swapaxes(b_s, 1, 2), HEAD_DIM, axis=2)
    gate_pat_all = _sample_gate_pattern(w_s, b_s)
    cache_k = jnp.transpose(cache_k, (0, 1, 3, 4, 2)).reshape(depth, n_pool, SB_WIDTH, CHUNK)
    cache_v = jnp.transpose(cache_v, (0, 1, 3, 4, 2)).reshape(depth, n_pool, SB_WIDTH, CHUNK)

    xp = x_prompt
    xs = x_sample.reshape(1, n_tok_s, D_MODEL)
    gp, ksm, vsm, gsm = [], [], [], []
    yp = ys = k_pages = v_pages = None
    for l in range(depth):
        final = l == depth - 1
        mods_p = mods[l, n_tok_s:][:, None, :]
        bias_p, bias_s, gate_b_p, gate_pat_s = bias_p_all[l], bias_s_all[l], gate_b_all[l], gate_pat_all[l]

        qb, u, gvn, k_pages, v_pages, kb, vb = _pre_call(
            xp, mods_p, None, l, g_mix[l], w_a, w_kv_t, gm_vnorm[l], 512,
            paged=(depth, k_pages, v_pages))
        qb_s, u_s, gvn_s, k_s, v_s, _, _ = _pre_call(
            xs, mods, l, l, g_mix[l], w_a, w_kv, gm_vnorm[l], n_tok_s)
        o_sb, o_sb_s = _attn_call(qb, kb, vb, bias_p, page_table, qb_s[0], k_s[0], v_s[0],
                                  cache_k, cache_v, l, bias_s, uu)

        outs = _post_call(xp, mods_p, None, l, o_sb, u, gvn, w_s[l], gate_b_p, gn_sb[l], gn_gm[l],
                          w_out_b, g_ff[l], w1_b, w2_b, g_final, 512, False, final)
        xp = outs[0]
        if final:
            yp = outs[1]
        gp.append(gvn[:, seq - CHUNK:].reshape(n_b, CHUNK, N_HEADS, HEAD_DIM))

        outs = _post_call(xs, mods, l, l, o_sb_s[None], u_s, gvn_s, gate_pat_s,
                          gate_b_p, gn_sb[l], gn_gm[l], w_out_b, g_ff[l], w1_b, w2_b, g_final,
                          n_tok_s, True, final)
        xs = outs[0]
        if final:
            ys = outs[1]
        ksm.append(k_s.reshape(n_seq, dec_seq, N_HEADS, HEAD_DIM))
        vsm.append(v_s.reshape(n_seq, dec_seq, N_HEADS, HEAD_DIM))
        gsm.append(gvn_s.reshape(n_seq, dec_seq, N_HEADS, HEAD_DIM))

    def unpage(pages):
        p = pages.reshape(depth, n_b, seq // CHUNK, N_HEADS, HEAD_DIM, CHUNK)
        return jnp.transpose(p, (0, 1, 2, 5, 3, 4))

    return (yp, ys.reshape(n_seq, dec_seq, D_MODEL), unpage(k_pages), unpage(v_pages), jnp.stack(gp),
            jnp.stack(ksm), jnp.stack(vsm), jnp.stack(gsm))
```

```python
import functools

import jax
import jax.numpy as jnp
from jax import lax
from jax.experimental import pallas as pl
from jax.experimental.pallas import tpu as pltpu

D_MODEL = 1024
SB_WIDTH = 512
GM_WIDTH = 512
HEAD_DIM = 64
N_HEADS = 8
N_PAIRS = 4
IN_WIDTH = 3 * SB_WIDTH + 2 * GM_WIDTH
D_FF = 4 * D_MODEL
CHUNK = 128
DEC_SEQ = 4
EPS = 1e-6
LOG2_E = 1.4426950408889634
MASKED = -1e30
LANES = 128
UNROLLS = (8, 4, 2, 1)
VMEM_LIMIT = 56 * 1024 * 1024

F32 = jnp.float32
BF16 = jnp.bfloat16


def _rms(x, g):
    ms = jnp.mean(x * x, axis=-1, keepdims=True)
    return x * lax.rsqrt(ms + EPS) * g


def _softplus2(z):
    return jnp.maximum(z, 0.0) + jnp.log(1.0 + jnp.exp2(-jnp.abs(z))) * LOG2_E


def _split_bf16(x):
    hi = x.astype(BF16)
    lo = (x - hi.astype(F32)).astype(BF16)
    return jnp.concatenate([hi, lo], axis=1)


def _dot(a, b):
    return jnp.dot(a, b, preferred_element_type=F32)


def _dot_nt(a, b):
    return lax.dot_general(a, b, (((1,), (1,)), ((), ())), preferred_element_type=F32)


def _ada_kernel(c_ref, w_ref, b_ref, o_ref):
    c = c_ref[...]
    s = c * jax.nn.sigmoid(c)
    o_ref[0] = _dot(s.astype(BF16), w_ref[0].astype(BF16)) + b_ref[0]


def _ada_call(c_all, w_ada, b_ada):
    depth, _, n_out = w_ada.shape
    rows = c_all.shape[0]
    tn = 2048
    return pl.pallas_call(
        _ada_kernel,
        grid=(depth, n_out // tn),
        in_specs=[
            pl.BlockSpec((rows, D_MODEL), lambda l, j: (0, 0)),
            pl.BlockSpec((1, D_MODEL, tn), lambda l, j: (l, 0, j)),
            pl.BlockSpec((1, 1, tn), lambda l, j: (l, 0, j)),
        ],
        out_specs=pl.BlockSpec((1, rows, tn), lambda l, j: (l, 0, j)),
        out_shape=jax.ShapeDtypeStruct((depth, rows, n_out), F32),
        compiler_params=pltpu.CompilerParams(
            dimension_semantics=("arbitrary", "arbitrary"), vmem_limit_bytes=VMEM_LIMIT),
        name="ada",
    )(c_all, w_ada, b_ada.reshape(depth, 1, n_out))


def _pre_kernel(x_ref, m_ref, gmix_ref, wa_ref, wkv_ref, gvnorm_ref, *refs, pages):
    q_ref, u_ref, gvn_ref, k_ref, v_ref, kb_ref, vb_ref = refs[-7:]
    x = x_ref[...]
    shift, scale = m_ref[:, :D_MODEL], m_ref[:, D_MODEL:]
    h = (_rms(x, gmix_ref[...]) * (1.0 + scale) + shift).astype(BF16)
    pa = _dot(h, wa_ref[...])
    q_ref[...] = (pa[:, :SB_WIDTH] * (HEAD_DIM ** -0.5 * LOG2_E)).astype(q_ref.dtype)
    u_ref[...] = pa[:, SB_WIDTH:SB_WIDTH + GM_WIDTH]
    gvn_ref[...] = _rms(pa[:, SB_WIDTH + GM_WIDTH:], gvnorm_ref[...])
    if pages:
        kv = _dot_nt(wkv_ref[...], h)
        for c in range(pages):
            blk = kv[:, c * CHUNK:(c + 1) * CHUNK]
            k_ref[c] = blk[:SB_WIDTH]
            v_ref[c] = blk[SB_WIDTH:]
            kb_ref[c] = blk[:SB_WIDTH].astype(BF16)
            vb_ref[c] = blk[SB_WIDTH:].astype(BF16)
    else:
        kv = _dot(h, wkv_ref[...])
        k_ref[...] = kv[:, :SB_WIDTH]
        v_ref[...] = kv[:, SB_WIDTH:]
        kb_ref[...] = kv[:, :SB_WIDTH].astype(BF16)
        vb_ref[...] = kv[:, SB_WIDTH:].astype(BF16)


def _mod_spec(rows, pair, lead):
    if lead is None:
        return pl.BlockSpec((None, rows, 2 * D_MODEL), lambda b, i: (b, 0, pair))
    return pl.BlockSpec((None, rows, 2 * D_MODEL), lambda b, i: (lead, 0, pair),
                        pipeline_mode=pl.Buffered(1))


def _layer_spec(w, layer):
    return pl.BlockSpec((None,) + w.shape[1:], lambda b, i: (layer,) + (0,) * (w.ndim - 1),
                        pipeline_mode=pl.Buffered(1))


def _pre_call(x, mods, mod_lead, layer, g_mix, w_a, w_kv, gm_vnorm, tm, paged=None):
    nb, t, _ = x.shape
    tok = lambda w: pl.BlockSpec((None, tm, w), lambda b, i: (b, i, 0))
    const = lambda shape: pl.BlockSpec(shape, lambda b, i: (0,) * len(shape))
    out_specs = [tok(SB_WIDTH), tok(GM_WIDTH), tok(GM_WIDTH)]
    out_shape = [jax.ShapeDtypeStruct((nb, t, SB_WIDTH), F32 if paged is None else BF16),
                 jax.ShapeDtypeStruct((nb, t, GM_WIDTH), F32),
                 jax.ShapeDtypeStruct((nb, t, GM_WIDTH), F32)]
    operands = [x, mods, g_mix.reshape(1, D_MODEL), w_a, w_kv, gm_vnorm.reshape(1, GM_WIDTH)]
    in_specs = [tok(D_MODEL), _mod_spec(1 if mod_lead is None else tm, 0, mod_lead),
                const((1, D_MODEL)), _layer_spec(w_a, layer), _layer_spec(w_kv, layer),
                const((1, GM_WIDTH))]
    aliases = {}
    if paged is None:
        pages = 0
        out_specs += [tok(SB_WIDTH)] * 4
        out_shape += [jax.ShapeDtypeStruct((nb, t, SB_WIDTH), dt) for dt in (F32, F32, BF16, BF16)]
    else:
        depth, k_all, v_all = paged
        pages = tm // CHUNK
        n_pages = t // CHUNK
        page_shape = (pages, SB_WIDTH, CHUNK)
        out_specs += [pl.BlockSpec((None, None) + page_shape, lambda b, i: (layer, b, i, 0, 0))] * 2
        out_specs += [pl.BlockSpec((None,) + page_shape, lambda b, i: (b, i, 0, 0))] * 2
        out_shape += [jax.ShapeDtypeStruct((depth, nb, n_pages, SB_WIDTH, CHUNK), F32)] * 2
        out_shape += [jax.ShapeDtypeStruct((nb, n_pages, SB_WIDTH, CHUNK), BF16)] * 2
        if k_all is not None:
            aliases = {len(operands): 3, len(operands) + 1: 4}
            operands += [k_all, v_all]
            in_specs += [pl.BlockSpec(memory_space=pl.ANY)] * 2
    return pl.pallas_call(
        functools.partial(_pre_kernel, pages=pages),
        grid=(nb, t // tm),
        in_specs=in_specs,
        out_specs=out_specs,
        out_shape=out_shape,
        input_output_aliases=aliases,
        compiler_params=pltpu.CompilerParams(
            dimension_semantics=("arbitrary", "arbitrary"), vmem_limit_bytes=VMEM_LIMIT),
        name="pre",
    )(*operands)


def _prompt_attention(qi, q_ref, k_ref, v_ref, bias_ref, uu_ref, o_ref,
                      q2_ref, z_ref, sp_ref, acc_ref, carry_ref):
    lane = lax.broadcasted_iota(jnp.int32, (CHUNK, LANES), 1)
    row = lax.broadcasted_iota(jnp.int32, (CHUNK, LANES), 0)
    first_head = lane < HEAD_DIM
    causal = lane < row
    causal2 = jnp.concatenate([causal, causal], axis=0)
    cols = [slice(hp * LANES, (hp + 1) * LANES) for hp in range(N_PAIRS)]

    qf = q_ref[...].astype(F32)
    for hp in range(N_PAIRS):
        qp = qf[:, cols[hp]]
        q2_ref[hp] = jnp.concatenate(
            [jnp.where(first_head, qp, 0.0), jnp.where(first_head, 0.0, qp)], axis=0).astype(BF16)
        acc_ref[hp] = jnp.zeros((2 * CHUNK, LANES), F32)
        carry_ref[hp] = jnp.zeros((2 * CHUNK, LANES), F32)

    def scores(kb):
        ks = k_ref[kb]
        return [_dot(q2_ref[hp], ks[cols[hp], :]) + bias_ref[hp] for hp in range(N_PAIRS)]

    def stash(zs):
        for hp in range(N_PAIRS):
            z_ref[hp] = zs[hp]
            sp_ref[hp] = _softplus2(zs[hp]).astype(BF16)

    def finish(kb):
        vs = v_ref[kb]
        zs = [z_ref[hp] for hp in range(N_PAIRS)]
        rs = [_dot(sp_ref[hp], uu_ref[:LANES, :]) for hp in range(N_PAIRS)]
        zs_next = scores(jnp.maximum(kb - 1, 0))
        es = []
        for hp in range(N_PAIRS):
            carry = carry_ref[hp]
            es.append(jnp.exp2(zs[hp] - (rs[hp][:, :LANES] + carry)).astype(BF16))
            carry_ref[hp] = carry + rs[hp][:, LANES:]
        for hp in range(N_PAIRS):
            acc_ref[hp] += _dot_nt(es[hp], vs[cols[hp], :])
        stash(zs_next)

    stash([jnp.where(causal2, z, MASKED) for z in scores(qi)])

    top = qi
    for width in UNROLLS:
        def body(i, c, top=top, width=width):
            for j in range(width):
                finish(top - width * i - j)
            return c

        trips = (top + 1) // width
        lax.fori_loop(0, trips, body, 0)
        top = top - trips * width

    for hp in range(N_PAIRS):
        a = acc_ref[hp]
        o_ref[:, cols[hp]] = jnp.where(first_head, a[:CHUNK], a[CHUNK:])


def _sample_attention(pair, half, q_ref, kn_ref, vn_ref, k_refs, v_refs, bias_ref, uu_ref, o_ref,
                      kpad_ref, vpad_ref):
    n_pages = len(k_refs)
    rows = DEC_SEQ * N_HEADS
    mine = slice(half * DEC_SEQ, (half + 1) * DEC_SEQ)

    lane = lax.broadcasted_iota(jnp.int32, (rows, SB_WIDTH), 1)
    row = lax.broadcasted_iota(jnp.int32, (rows, SB_WIDTH), 0)
    own_head = lane // HEAD_DIM == row % N_HEADS
    qf = q_ref[pair, :][mine]
    qrep = jnp.broadcast_to(qf[:, None, :], (DEC_SEQ, N_HEADS, SB_WIDTH)).reshape(rows, SB_WIDTH)
    q2 = jnp.where(own_head, qrep, 0.0).astype(BF16)

    kpad_ref[0:DEC_SEQ, :] = kn_ref[pair, :][mine]
    vpad_ref[0:DEC_SEQ, :] = vn_ref[pair, :][mine]
    col = lax.broadcasted_iota(jnp.int32, (rows, LANES), 1)
    qpos = lax.broadcasted_iota(jnp.int32, (rows, LANES), 0) // N_HEADS
    bias = bias_ref[...]

    zs = [jnp.where(col < qpos, _dot_nt(q2, kpad_ref[...].astype(BF16)) + bias, MASKED)]
    for p in reversed(range(n_pages)):
        zs.append(_dot(q2, k_refs[p][...].astype(BF16)) + bias)
    sp = jnp.concatenate([_split_bf16(_softplus2(z)) for z in zs], axis=0)
    r = _dot(sp, uu_ref[...])

    carry = jnp.zeros((rows, LANES), F32)
    acc = jnp.zeros((rows, SB_WIDTH), F32)
    for i, z in enumerate(zs):
        ri = r[i * rows:(i + 1) * rows]
        e = jnp.exp2(z - (ri[:, :LANES] + carry)).astype(BF16)
        carry = carry + ri[:, LANES:]
        if i == 0:
            acc = acc + _dot(e, vpad_ref[...].astype(BF16))
        else:
            acc = acc + _dot_nt(e, v_refs[n_pages - i][...].astype(BF16))

    own = jnp.where(own_head, acc, 0.0)
    out = jnp.sum(jnp.concatenate([own, own], axis=0).reshape(2 * DEC_SEQ, N_HEADS, SB_WIDTH), axis=1)
    tile_row = lax.broadcasted_iota(jnp.int32, (2 * DEC_SEQ, SB_WIDTH), 0)
    other = 0.0 if half == 0 else o_ref[pair, :]
    o_ref[pair, :] = jnp.where(tile_row // DEC_SEQ == half, out, other)


def _attn_kernel(pt_ref, qp_ref, kp_ref, vp_ref, biasp_ref, uu_ref, qs_ref, kn_ref, vn_ref,
                 ck_ref, cv_ref, biass_ref, op_ref, os_ref,
                 q2_ref, z_ref, sp_ref, acc_ref, carry_ref, kpad_ref, vpad_ref, kbuf_ref, vbuf_ref, sem,
                 *, layer, n_pages):
    seq = pl.program_id(0) * pl.num_programs(1) + pl.program_id(1)
    n_seq = pl.num_programs(0) * pl.num_programs(1)
    slot = seq % 2

    def page_copies(s, into):
        copies = []
        for p in range(n_pages):
            page = pt_ref[s, p]
            copies.append(pltpu.make_async_copy(ck_ref.at[layer, page], kbuf_ref.at[into, p], sem.at[into]))
            copies.append(pltpu.make_async_copy(cv_ref.at[layer, page], vbuf_ref.at[into, p], sem.at[into]))
        return copies

    @pl.when(seq == 0)
    def _():
        kpad_ref[...] = jnp.zeros_like(kpad_ref)
        vpad_ref[...] = jnp.zeros_like(vpad_ref)
        for c in page_copies(0, 0):
            c.start()

    ahead = jnp.minimum(seq + 1, n_seq - 1)
    for c in page_copies(ahead, 1 - slot):
        c.start()

    _prompt_attention(pl.program_id(1), qp_ref, kp_ref, vp_ref, biasp_ref, uu_ref, op_ref,
                      q2_ref, z_ref, sp_ref, acc_ref, carry_ref)

    @pl.when(seq == n_seq - 1)
    def _():
        for c in page_copies(ahead, 1 - slot):
            c.wait()

    for this in (0, 1):
        @pl.when(slot == this)
        def _(this=this):
            for c in page_copies(seq, this):
                c.wait()
            pair = pl.ds(pl.multiple_of(seq // 2 * (2 * DEC_SEQ), 2 * DEC_SEQ), 2 * DEC_SEQ)
            _sample_attention(pair, this, qs_ref, kn_ref, vn_ref,
                              [kbuf_ref.at[this, p] for p in range(n_pages)],
                              [vbuf_ref.at[this, p] for p in range(n_pages)],
                              biass_ref, uu_ref, os_ref, kpad_ref, vpad_ref)


def _attn_call(qb, kb, vb, bias_p, page_table, qs, k_new, v_new, cache_k, cache_v, layer, bias_s, uu):
    nb, t, _ = qb.shape
    blocks = t // CHUNK
    n_seq, n_pages = page_table.shape
    assert n_seq == nb * blocks, "one sample sequence rides along with each prompt query block"
    rows = DEC_SEQ * N_HEADS
    stacked = (N_PAIRS, 2 * CHUNK, LANES)
    slots = (2, n_pages, SB_WIDTH, CHUNK)
    new = lambda: pl.BlockSpec((n_seq * DEC_SEQ, SB_WIDTH), lambda b, i, pt: (0, 0))

    grid_spec = pltpu.PrefetchScalarGridSpec(
        num_scalar_prefetch=1,
        grid=(nb, blocks),
        in_specs=[
            pl.BlockSpec((None, CHUNK, SB_WIDTH), lambda b, i, pt: (b, i, 0)),
            pl.BlockSpec((None,) + kb.shape[1:], lambda b, i, pt: (b, 0, 0, 0)),
            pl.BlockSpec((None,) + vb.shape[1:], lambda b, i, pt: (b, 0, 0, 0)),
            pl.BlockSpec(stacked, lambda b, i, pt: (0, 0, 0)),
            pl.BlockSpec((2 * LANES, 2 * LANES), lambda b, i, pt: (0, 0)),
            new(), new(), new(),
            pl.BlockSpec(memory_space=pl.ANY), pl.BlockSpec(memory_space=pl.ANY),
            pl.BlockSpec((rows, LANES), lambda b, i, pt: (0, 0)),
        ],
        out_specs=[
            pl.BlockSpec((None, CHUNK, SB_WIDTH), lambda b, i, pt: (b, i, 0)),
            new(),
        ],
        scratch_shapes=[
            pltpu.VMEM(stacked, BF16),
            pltpu.VMEM(stacked, F32),
            pltpu.VMEM(stacked, BF16),
            pltpu.VMEM(stacked, F32),
            pltpu.VMEM(stacked, F32),
            pltpu.VMEM((CHUNK, SB_WIDTH), F32),
            pltpu.VMEM((CHUNK, SB_WIDTH), F32),
            pltpu.VMEM(slots, F32),
            pltpu.VMEM(slots, F32),
            pltpu.SemaphoreType.DMA((2,)),
        ],
    )
    return pl.pallas_call(
        functools.partial(_attn_kernel, layer=layer, n_pages=n_pages),
        grid_spec=grid_spec,
        out_shape=[jax.ShapeDtypeStruct((nb, t, SB_WIDTH), F32),
                   jax.ShapeDtypeStruct((n_seq * DEC_SEQ, SB_WIDTH), F32)],
        compiler_params=pltpu.CompilerParams(
            dimension_semantics=("arbitrary", "arbitrary"), vmem_limit_bytes=VMEM_LIMIT),
        name="attn",
    )(page_table, qb, kb, vb, bias_p, uu, qs, k_new, v_new, cache_k, cache_v, bias_s)


def _gate_prompt(gvn, ws_ref, gbias):
    tm = gvn.shape[0]
    lane = lax.broadcasted_iota(jnp.int32, (CHUNK, LANES), 1)
    row = lax.broadcasted_iota(jnp.int32, (CHUNK, LANES), 0)
    first_group = lane < HEAD_DIM
    keep = lane <= row
    out_cols = []
    for gp in range(N_PAIRS):
        w_pair = jnp.concatenate(
            [jnp.where(keep, ws_ref[2 * gp], 0.0), jnp.where(keep, ws_ref[2 * gp + 1], 0.0)],
            axis=1).astype(BF16)
        chunks = []
        for c in range(tm // CHUNK):
            r = gvn[c * CHUNK:(c + 1) * CHUNK, gp * LANES:(gp + 1) * LANES]
            r2 = jnp.concatenate(
                [jnp.where(first_group, r, 0.0), jnp.where(first_group, 0.0, r)], axis=0).astype(BF16)
            chunks.append(_dot(w_pair, r2) + gbias[:, gp * LANES:(gp + 1) * LANES])
        out_cols.append(jnp.concatenate(chunks, axis=0))
    return jnp.concatenate(out_cols, axis=1)


def _gate_sample(gvn, pat_ref):
    tm = gvn.shape[0]
    g3 = gvn.reshape(tm // 8, 8, GM_WIDTH)
    out = g3 * pat_ref[0][None] + pat_ref[DEC_SEQ][None]
    for k in range(1, DEC_SEQ):
        out = out + pltpu.roll(g3, k, axis=1) * pat_ref[k][None]
    return out.reshape(tm, GM_WIDTH)


def _post_kernel(x_ref, ma_ref, mb_ref, osb_ref, u_ref, gvn_ref, gate_w_ref, gate_b_ref, gnsb_ref,
                 gngm_ref, wout_ref, gff_ref, w1_ref, w2_ref, gfin_ref, *out_refs, sample, final):
    gate1, shift2 = ma_ref[:, :D_MODEL], ma_ref[:, D_MODEL:]
    scale2, gate2 = mb_ref[:, :D_MODEL], mb_ref[:, D_MODEL:]
    x = x_ref[...]
    o_sb = _rms(osb_ref[...], gnsb_ref[...])
    if sample:
        mixed = _gate_sample(gvn_ref[...], gate_w_ref)
    else:
        mixed = _gate_prompt(gvn_ref[...], gate_w_ref, gate_b_ref[...])
    o_gm = _rms(u_ref[...] * mixed, gngm_ref[...])
    cat = jnp.concatenate([o_sb, o_gm], axis=-1).astype(BF16)
    x = x + gate1 * _dot(cat, wout_ref[...])
    h = (_rms(x, gff_ref[...]) * (1.0 + scale2) + shift2).astype(BF16)
    ff = None
    n_split = 4
    cw = D_FF // n_split
    for c in range(n_split):
        a = jnp.maximum(_dot(h, w1_ref[:, c * cw:(c + 1) * cw]), 0.0)
        part = _dot((a * a).astype(BF16), w2_ref[c * cw:(c + 1) * cw, :])
        ff = part if ff is None else ff + part
    x = x + gate2 * ff
    out_refs[0][...] = x
    if final:
        out_refs[1][...] = _rms(x, gfin_ref[...])


def _post_call(x, mods, mod_lead, layer, o_sb, u, gvn, gate_w, gate_b, gn_sb, gn_gm, w_out, g_ff,
               w_ff1, w_ff2, g_final, tm, sample, final):
    nb, t, _ = x.shape
    mod_rows = 1 if mod_lead is None else tm
    tok = lambda w: pl.BlockSpec((None, tm, w), lambda b, i: (b, i, 0))

    def const(shape):
        return pl.BlockSpec(shape, lambda b, i: (0,) * len(shape), pipeline_mode=pl.Buffered(1))

    n_out = 2 if final else 1
    outs = pl.pallas_call(
        functools.partial(_post_kernel, sample=sample, final=final),
        grid=(nb, t // tm),
        in_specs=[
            tok(D_MODEL),
            _mod_spec(mod_rows, 1, mod_lead), _mod_spec(mod_rows, 2, mod_lead),
            tok(SB_WIDTH), tok(GM_WIDTH), tok(GM_WIDTH),
            const(gate_w.shape), const(gate_b.shape),
            const((1, SB_WIDTH)), const((1, GM_WIDTH)),
            _layer_spec(w_out, layer),
            const((1, D_MODEL)),
            _layer_spec(w_ff1, layer), _layer_spec(w_ff2, layer),
            const((1, D_MODEL)),
        ],
        out_specs=[tok(D_MODEL)] * n_out,
        out_shape=[jax.ShapeDtypeStruct((nb, t, D_MODEL), F32)] * n_out,
        compiler_params=pltpu.CompilerParams(
            dimension_semantics=("arbitrary", "arbitrary"), vmem_limit_bytes=VMEM_LIMIT),
        name="post_sample" if sample else "post_prompt",
    )(x, mods, mods, o_sb, u, gvn, gate_w, gate_b, gn_sb.reshape(1, SB_WIDTH),
      gn_gm.reshape(1, GM_WIDTH), w_out, g_ff.reshape(1, D_MODEL), w_ff1, w_ff2,
      g_final.reshape(1, D_MODEL))
    return outs


def _suffix_sum_matrix():
    j = jnp.arange(LANES)[:, None]
    s = jnp.arange(LANES)[None, :]
    half = jnp.concatenate([(j >= s).astype(BF16), jnp.ones((LANES, LANES), BF16)], axis=1)
    return jnp.concatenate([half, half], axis=0)


def _sample_gate_pattern(w_s, b_s):
    w4 = w_s[:, :, :DEC_SEQ, :DEC_SEQ]
    k = jnp.arange(DEC_SEQ)[:, None, None]
    t = jnp.arange(DEC_SEQ)[None, :, None]
    s = jnp.arange(DEC_SEQ)[None, None, :]
    pick = (s == t - k).astype(F32)
    pat = jnp.sum(w4[:, None] * pick[None, :, None], axis=-1)
    pat = jnp.concatenate([pat, b_s[:, None, :, :DEC_SEQ]], axis=1)
    pat = jnp.swapaxes(pat, 2, 3)
    pat = jnp.concatenate([pat, pat], axis=2)
    return jnp.repeat(pat, HEAD_DIM, axis=-1)


def kernel(x_prompt, x_sample, cache_k, cache_v, page_table, c_prompt, c_sample, w_ada, b_ada, g_mix,
           w_in, sb_bias, gm_vnorm, w_s, b_s, gn_sb, gn_gm, w_out, g_ff, w_ff1, w_ff2, g_final):
    depth = w_in.shape[0]
    n_b, seq, _ = x_prompt.shape
    n_seq, dec_seq, _ = x_sample.shape
    n_pool = cache_k.shape[1]
    n_tok_s = n_seq * dec_seq

    mods = _ada_call(jnp.concatenate([jnp.repeat(c_sample, dec_seq, axis=0), c_prompt], axis=0),
                     w_ada, b_ada)
    uu = _suffix_sum_matrix()
    w_a = jnp.concatenate([w_in[:, :, :SB_WIDTH], w_in[:, :, 3 * SB_WIDTH:]], axis=2).astype(BF16)
    w_kv = w_in[:, :, SB_WIDTH:3 * SB_WIDTH].astype(BF16)
    w_kv_t = jnp.swapaxes(w_kv, 1, 2)
    w_out_b = w_out.astype(BF16)
    w1_b = w_ff1.astype(BF16)
    w2_b = w_ff2.astype(BF16)
    bias2 = sb_bias * LOG2_E
    bias_p_all = jnp.broadcast_to(jnp.repeat(bias2, CHUNK, axis=1).reshape(depth, N_PAIRS, 2 * CHUNK, 1),
                                  (depth, N_PAIRS, 2 * CHUNK, LANES))
    bias_s_all = jnp.broadcast_to(jnp.tile(bias2, (1, dec_seq))[:, :, None],
                                  (depth, dec_seq * N_HEADS, LANES))
    gate_b_all = jnp.repeat(jnp.swapaxes(b_s, 1, 2), HEAD_DIM, axis=2)
    gate_pat_all = _sample_gate_pattern(w_s, b_s)
    cache_k = jnp.transpose(cache_k, (0, 1, 3, 4, 2)).reshape(depth, n_pool, SB_WIDTH, CHUNK)
    cache_v = jnp.transpose(cache_v, (0, 1, 3, 4, 2)).reshape(depth, n_pool, SB_WIDTH, CHUNK)

    xp = x_prompt
    xs = x_sample.reshape(1, n_tok_s, D_MODEL)
    gp, ksm, vsm, gsm = [], [], [], []
    yp = ys = k_pages = v_pages = None
    for l in range(depth):
        final = l == depth - 1
        mods_p = mods[l, n_tok_s:][:, None, :]
        bias_p, bias_s, gate_b_p, gate_pat_s = bias_p_all[l], bias_s_all[l], gate_b_all[l], gate_pat_all[l]

        qb, u, gvn, k_pages, v_pages, kb, vb = _pre_call(
            xp, mods_p, None, l, g_mix[l], w_a, w_kv_t, gm_vnorm[l], 1024,
            paged=(depth, k_pages, v_pages))
        qb_s, u_s, gvn_s, k_s, v_s, _, _ = _pre_call(
            xs, mods, l, l, g_mix[l], w_a, w_kv, gm_vnorm[l], n_tok_s)
        o_sb, o_sb_s = _attn_call(qb, kb, vb, bias_p, page_table, qb_s[0], k_s[0], v_s[0],
                                  cache_k, cache_v, l, bias_s, uu)

        outs = _post_call(xp, mods_p, None, l, o_sb, u, gvn, w_s[l], gate_b_p, gn_sb[l], gn_gm[l],
                          w_out_b, g_ff[l], w1_b, w2_b, g_final, 512, False, final)
        xp = outs[0]
        if final:
            yp = outs[1]
        gp.append(gvn[:, seq - CHUNK:].reshape(n_b, CHUNK, N_HEADS, HEAD_DIM))

        outs = _post_call(xs, mods, l, l, o_sb_s[None], u_s, gvn_s, gate_pat_s,
                          gate_b_p, gn_sb[l], gn_gm[l], w_out_b, g_ff[l], w1_b, w2_b, g_final,
                          n_tok_s, True, final)
        xs = outs[0]
        if final:
            ys = outs[1]
        ksm.append(k_s.reshape(n_seq, dec_seq, N_HEADS, HEAD_DIM))
        vsm.append(v_s.reshape(n_seq, dec_seq, N_HEADS, HEAD_DIM))
        gsm.append(gvn_s.reshape(n_seq, dec_seq, N_HEADS, HEAD_DIM))

    def unpage(pages):
        p = pages.reshape(depth, n_b, seq // CHUNK, N_HEADS, HEAD_DIM, CHUNK)
        return jnp.transpose(p, (0, 1, 2, 5, 3, 4))

    return (yp, ys.reshape(n_seq, dec_seq, D_MODEL), unpage(k_pages), unpage(v_pages), jnp.stack(gp),
            jnp.stack(ksm), jnp.stack(vsm), jnp.stack(gsm))
```
